```python
import math
import jax, jax.numpy as jnp
from jax import lax
import numpy as np

D_MODEL = 4096
BATCH = 1
SEQ = 16384
DEPTH = 2

HEAD_DIM = 128
MIX_WIDTH = D_MODEL
A_WIDTH = MIX_WIDTH // 2
A_HEADS = A_WIDTH // HEAD_DIM
B_WIDTH = MIX_WIDTH - A_WIDTH
CONV_WIDTH = 3
A_BRANCHES = ((128, 1), (512, 4), (2048, 16))
QBLK = 128
ROT_DIM = HEAD_DIM // 4
ROPE_THETA = 500000.0
DIFF_HEADS = D_MODEL // (2 * HEAD_DIM)
DIFF_V_DIM = 2 * HEAD_DIM
D_FF = 256 * ((8 * D_MODEL // 3 + 255) // 256)
N_EXPERTS = 8
TOP_K = 2
D_FF_EXPERT = 3 * D_MODEL // 2
NORM_EPS = 1e-6
ATTN_SCALE = HEAD_DIM ** -0.5

kernel_name = "hybrid_dilated_conv_diffattn_moe"


def rms_norm(x, g):
    xf = x.astype(jnp.float32)
    y = xf * lax.rsqrt(jnp.mean(xf * xf, axis=-1, keepdims=True) + NORM_EPS)
    return (y * g.astype(jnp.float32)).astype(x.dtype)


def rotary_tables(seq):
    pos = jnp.arange(seq, dtype=jnp.float32)
    inv = ROPE_THETA ** (-jnp.arange(0, ROT_DIM, 2, dtype=jnp.float32) / ROT_DIM)
    ang = pos[:, None] * inv[None, :]
    return jnp.cos(ang), jnp.sin(ang)


def apply_rotary(x, cos, sin):
    half = ROT_DIM // 2
    xf = x.astype(jnp.float32)
    c = cos[None, :, None, :]
    s = sin[None, :, None, :]
    x1 = xf[..., :half]
    x2 = xf[..., half:ROT_DIM]
    out = jnp.concatenate([x1 * c - x2 * s, x2 * c + x1 * s, xf[..., ROT_DIM:]], axis=-1)
    return out.astype(x.dtype)


def dilated_window_branch(q, k, v, window, dilation):
    n_back = window // dilation
    assert n_back <= QBLK
    B, S, H, Dh = q.shape
    unit = dilation * QBLK
    s_pad = -(-S // unit) * unit
    L = s_pad // dilation
    nb = L // QBLK

    def split(a):
        a = jnp.pad(a, ((0, 0), (0, s_pad - S), (0, 0), (0, 0)))
        return a.reshape(B, nb, QBLK, dilation, H, a.shape[-1])

    def with_prev(a):
        prev = jnp.pad(a, ((0, 0), (1, 0), (0, 0), (0, 0), (0, 0), (0, 0)))[:, :-1]
        return jnp.concatenate([prev, a], axis=2)

    qs = split(q)
    kk = with_prev(split(k))
    vv = with_prev(split(v))
    s = jnp.einsum('bnirhd,bnjrhd->bnrhij', qs, kk).astype(jnp.float32) * ATTN_SCALE
    i = jnp.arange(QBLK)[:, None]
    j = jnp.arange(2 * QBLK)[None, :]
    n = jnp.arange(nb)[:, None, None]
    dist = QBLK + i - j
    mask = (dist >= 0) & (dist <= n_back) & (n * QBLK + j >= QBLK)
    s = jnp.where(mask[None, :, None, None], s, -jnp.inf)
    m = jnp.max(s, axis=-1, keepdims=True)
    p = jnp.exp(s - m)
    den = jnp.sum(p, axis=-1, keepdims=True)
    o = jnp.einsum('bnrhij,bnjrhd->bnirhd', (p / den).astype(v.dtype), vv)
    lse = (m + jnp.log(den))[..., 0]
    o = o.reshape(B, s_pad, H, Dh)[:, :S]
    lse = lse.transpose(0, 1, 4, 2, 3).reshape(B, s_pad, H)[:, :S]
    return o, lse


def dilated_mixture_attention(q, k, v):
    outs, lses = [], []
    for window, dilation in A_BRANCHES:
        o, l = dilated_window_branch(q, k, v, window, dilation)
        outs.append(o)
        lses.append(l)
    wts = jax.nn.softmax(jnp.stack(lses, axis=0), axis=0)
    o = jnp.einsum('gbsh,gbshd->bshd', wts, jnp.stack(outs, axis=0).astype(jnp.float32))
    return o.astype(q.dtype)


def causal_short_conv(u, w):
    C = u.shape[-1]
    return lax.conv_general_dilated(
        u, w[:, None, :].astype(u.dtype), window_strides=(1,),
        padding=[(CONV_WIDTH - 1, 0)], dimension_numbers=('NWC', 'WIO', 'NWC'),
        feature_group_count=C)


def swiglu(h, w_gate, w_up, w_down):
    return (jax.nn.silu(h @ w_gate) * (h @ w_up)) @ w_down


def even_mixer(h, w_in, conv_w, w_out, cos, sin):
    B, S, _ = h.shape
    proj = h @ w_in
    q, k, v, gate_b, gate_c, hb = jnp.split(
        proj, np.cumsum([A_WIDTH] * 3 + [B_WIDTH] * 2).tolist(), axis=-1)
    q = apply_rotary(q.reshape(B, S, A_HEADS, HEAD_DIM), cos, sin)
    k = apply_rotary(k.reshape(B, S, A_HEADS, HEAD_DIM), cos, sin)
    v = v.reshape(B, S, A_HEADS, HEAD_DIM)
    a_out = dilated_mixture_attention(q, k, v).reshape(B, S, A_WIDTH)
    b_out = gate_b * causal_short_conv(gate_c * hb, conv_w)
    return jnp.concatenate([a_out, b_out], axis=-1) @ w_out


def differential_attention(q, k, v, lam, subln_g, lambda_init):
    B, S, H2, Dh = q.shape
    H = H2 // 2
    nb = S // QBLK
    qb = q.reshape(B, nb, QBLK, H2, Dh).transpose(1, 0, 3, 2, 4)
    kt = k.transpose(0, 2, 1, 3)
    vt = v.transpose(0, 2, 1, 3)
    key_pos = jnp.arange(S)

    def block(args):
        qi, bi = args
        s = jnp.einsum('bhid,bhjd->bhij', qi, kt).astype(jnp.float32) * ATTN_SCALE
        q_pos = bi * QBLK + jnp.arange(QBLK)
        s = jnp.where(key_pos[None, :] <= q_pos[:, None], s, -jnp.inf)
        p = jax.nn.softmax(s, axis=-1).reshape(B, H, 2, QBLK, S)
        a = p[:, :, 0] - lam * p[:, :, 1]
        o = jnp.einsum('bhij,bhjd->bhid', a.astype(vt.dtype), vt)
        return rms_norm(o, subln_g) * (1.0 - lambda_init)

    out = lax.map(block, (qb, jnp.arange(nb)))
    return out.transpose(1, 0, 3, 2, 4).reshape(B, S, H * 2 * Dh)


def odd_mixer(h, w_qkv, lq1, lk1, lq2, lk2, subln_g, w_out, cos, sin, layer):
    B, S, _ = h.shape
    lambda_init = 0.8 - 0.6 * math.exp(-0.3 * layer)
    proj = h @ w_qkv
    q, k, v = jnp.split(proj, [D_MODEL, 2 * D_MODEL], axis=-1)
    q = apply_rotary(q.reshape(B, S, 2 * DIFF_HEADS, HEAD_DIM), cos, sin)
    k = apply_rotary(k.reshape(B, S, 2 * DIFF_HEADS, HEAD_DIM), cos, sin)
    v = v.reshape(B, S, DIFF_HEADS, DIFF_V_DIM)
    f32 = jnp.float32
    lam = (jnp.exp(jnp.sum(lq1.astype(f32) * lk1.astype(f32)))
           - jnp.exp(jnp.sum(lq2.astype(f32) * lk2.astype(f32))) + lambda_init)
    o = differential_attention(q, k, v, lam, subln_g, lambda_init)
    return o @ w_out


def moe_swiglu(h, router_w, w_gate, w_up, w_down):
    B, S, D = h.shape
    xf = h.reshape(B * S, D)
    logits = (xf @ router_w).astype(jnp.float32)
    top_vals, top_idx = lax.top_k(logits, TOP_K)
    g = jax.nn.softmax(top_vals, axis=-1)
    gates = jnp.sum(jax.nn.one_hot(top_idx, N_EXPERTS, dtype=jnp.float32) * g[..., None], axis=1)
    gates = gates.astype(h.dtype)
    out = jnp.zeros_like(xf)
    for e in range(N_EXPERTS):
        out = out + gates[:, e:e + 1] * swiglu(xf, w_gate[e], w_up[e], w_down[e])
    return out.reshape(B, S, D)


def even_layer(x, g_pre, w_in, conv_w, w_out, g_post, g_pre_f, wg, wu, wd, g_post_f, cos, sin):
    x = x + rms_norm(even_mixer(rms_norm(x, g_pre), w_in, conv_w, w_out, cos, sin), g_post)
    x = x + rms_norm(swiglu(rms_norm(x, g_pre_f), wg, wu, wd), g_post_f)
    return x


def odd_layer(x, g_pre, w_qkv, lq1, lk1, lq2, lk2, subln_g, w_out, g_post, g_pre_f,
              router_w, wg, wu, wd, g_post_f, cos, sin, layer):
    x = x + rms_norm(odd_mixer(rms_norm(x, g_pre), w_qkv, lq1, lk1, lq2, lk2, subln_g,
                               w_out, cos, sin, layer), g_post)
    x = x + rms_norm(moe_swiglu(rms_norm(x, g_pre_f), router_w, wg, wu, wd), g_post_f)
    return x


def setup_inputs(seed: int = 0) -> dict:
    key = jax.random.key(seed)
    ks = iter(jax.random.split(key, 32))

    def w(shape, fan_in):
        return jax.random.normal(next(ks), shape, jnp.float32) * fan_in ** -0.5

    def gain(n):
        return 1.0 + 0.02 * jax.random.normal(next(ks), (n,), jnp.float32)

    in0 = 3 * A_WIDTH + 3 * B_WIDTH
    return {
        "x": jax.random.normal(next(ks), (BATCH, SEQ, D_MODEL), jnp.float32),
        "l0_norm_pre_mix": gain(D_MODEL),
        "l0_w_in": w((D_MODEL, in0), D_MODEL),
        "l0_conv_w": w((CONV_WIDTH, B_WIDTH), CONV_WIDTH),
        "l0_w_out": w((MIX_WIDTH, D_MODEL), MIX_WIDTH),
        "l0_norm_post_mix": gain(D_MODEL),
        "l0_norm_pre_ffn": gain(D_MODEL),
        "l0_ffn_w_gate": w((D_MODEL, D_FF), D_MODEL),
        "l0_ffn_w_up": w((D_MODEL, D_FF), D_MODEL),
        "l0_ffn_w_down": w((D_FF, D_MODEL), D_FF),
        "l0_norm_post_ffn": gain(D_MODEL),
        "l1_norm_pre_mix": gain(D_MODEL),
        "l1_w_qkv": w((D_MODEL, 3 * D_MODEL), D_MODEL),
        "l1_lambda_q1": 0.1 * jax.random.normal(next(ks), (HEAD_DIM,), jnp.float32),
        "l1_lambda_k1": 0.1 * jax.random.normal(next(ks), (HEAD_DIM,), jnp.float32),
        "l1_lambda_q2": 0.1 * jax.random.normal(next(ks), (HEAD_DIM,), jnp.float32),
        "l1_lambda_k2": 0.1 * jax.random.normal(next(ks), (HEAD_DIM,), jnp.float32),
        "l1_subln_g": gain(DIFF_V_DIM),
        "l1_w_out": w((D_MODEL, D_MODEL), D_MODEL),
        "l1_norm_post_mix": gain(D_MODEL),
        "l1_norm_pre_ffn": gain(D_MODEL),
        "l1_router_w": w((D_MODEL, N_EXPERTS), D_MODEL),
        "l1_moe_w_gate": w((N_EXPERTS, D_MODEL, D_FF_EXPERT), D_MODEL),
        "l1_moe_w_up": w((N_EXPERTS, D_MODEL, D_FF_EXPERT), D_MODEL),
        "l1_moe_w_down": w((N_EXPERTS, D_FF_EXPERT, D_MODEL), D_FF_EXPERT),
        "l1_norm_post_ffn": gain(D_MODEL),
    }


def reference(x,
              l0_norm_pre_mix, l0_w_in, l0_conv_w, l0_w_out, l0_norm_post_mix,
              l0_norm_pre_ffn, l0_ffn_w_gate, l0_ffn_w_up, l0_ffn_w_down, l0_norm_post_ffn,
              l1_norm_pre_mix, l1_w_qkv, l1_lambda_q1, l1_lambda_k1, l1_lambda_q2,
              l1_lambda_k2, l1_subln_g, l1_w_out, l1_norm_post_mix, l1_norm_pre_ffn,
              l1_router_w, l1_moe_w_gate, l1_moe_w_up, l1_moe_w_down, l1_norm_post_ffn):
    cos, sin = rotary_tables(x.shape[1])
    layer_params = [
        (l0_norm_pre_mix, l0_w_in, l0_conv_w, l0_w_out, l0_norm_post_mix,
         l0_norm_pre_ffn, l0_ffn_w_gate, l0_ffn_w_up, l0_ffn_w_down, l0_norm_post_ffn),
        (l1_norm_pre_mix, l1_w_qkv, l1_lambda_q1, l1_lambda_k1, l1_lambda_q2,
         l1_lambda_k2, l1_subln_g, l1_w_out, l1_norm_post_mix, l1_norm_pre_ffn,
         l1_router_w, l1_moe_w_gate, l1_moe_w_up, l1_moe_w_down, l1_norm_post_ffn),
    ]
    for layer in range(DEPTH):
        if layer % 2 == 0:
            x = even_layer(x, *layer_params[layer], cos, sin)
        else:
            x = odd_layer(x, *layer_params[layer], cos, sin, layer)
    return x
```

```python
import functools
import math

import jax
import jax.numpy as jnp
from jax import lax
from jax.experimental import pallas as pl
from jax.experimental.pallas import tpu as pltpu

HEAD_DIM = 128
ROT_DIM = HEAD_DIM // 4
ROPE_THETA = 500000.0
QBLK = 128
A_BRANCHES = ((128, 1), (512, 4), (2048, 16))
DIL_UNIT = max(r for _, r in A_BRANCHES) * QBLK
CONV_WIDTH = 3
TOP_K = 2
NORM_EPS = 1e-6
ATTN_SCALE = HEAD_DIM ** -0.5

LANES = 128
SUBLANES = 8
VMEM_BYTES_V7X = 64 * 1024 * 1024
VMEM_LIMIT = VMEM_BYTES_V7X - 8 * 1024 * 1024

F32 = jnp.float32
BF16 = jnp.bfloat16


def _tile(n, pref, align=LANES):
    if n <= pref:
        return n
    t = (pref // align) * align
    while t >= align:
        if n % t == 0:
            return t
        t -= align
    raise ValueError(f"no {align}-aligned tile <= {pref} divides {n}")


def _params(*sem):
    return pltpu.CompilerParams(dimension_semantics=sem, vmem_limit_bytes=VMEM_LIMIT)


def _rms(x, g):
    ms = jnp.mean(x * x, axis=-1, keepdims=True)
    return x * lax.rsqrt(ms + NORM_EPS) * g


def _rmsnorm_kernel(x_ref, g_ref, h_ref):
    h_ref[...] = _rms(x_ref[...], g_ref[...]).astype(h_ref.dtype)


def rmsnorm_cast(x, g):
    S, D = x.shape
    tr = _tile(S, 256, SUBLANES)
    return pl.pallas_call(
        _rmsnorm_kernel,
        grid=(S // tr,),
        in_specs=[pl.BlockSpec((tr, D), lambda i: (i, 0)),
                  pl.BlockSpec((1, D), lambda i: (0, 0))],
        out_specs=pl.BlockSpec((tr, D), lambda i: (i, 0)),
        out_shape=jax.ShapeDtypeStruct((S, D), BF16),
        compiler_params=_params("parallel"),
        name="rmsnorm_cast",
    )(x, g.reshape(1, D))


def _to_row_slabs(ref, val):
    rows, d = val.shape
    chunks = d // LANES
    for s in range(chunks):
        ref[pl.ds(s, rows, stride=chunks), :] = val[:, s * LANES:(s + 1) * LANES]


def _from_row_slabs(ref, rows, s):
    chunks = ref.shape[0] // rows
    return ref[pl.ds(s, rows, stride=chunks), :]


def _resid_norm_kernel(y_ref, x_ref, gpost_ref, gpre_ref, xo_ref, h_ref, *, slabs):
    xn = x_ref[...] + _rms(y_ref[...], gpost_ref[...])
    if slabs:
        _to_row_slabs(xo_ref, xn)
    else:
        xo_ref[...] = xn
    h_ref[...] = _rms(xn, gpre_ref[...]).astype(h_ref.dtype)


def resid_norm(y, x, g_post, g_pre, *, slabs=False):
    S, D = x.shape
    tr = _tile(S, 256, SUBLANES)
    row = pl.BlockSpec((tr, D), lambda i: (i, 0))
    vec = pl.BlockSpec((1, D), lambda i: (0, 0))
    if slabs:
        chunks = D // LANES
        x_spec = pl.BlockSpec((tr * chunks, LANES), lambda i: (i, 0))
        x_shape = jax.ShapeDtypeStruct((S * chunks, LANES), F32)
    else:
        x_spec, x_shape = row, jax.ShapeDtypeStruct((S, D), F32)
    return pl.pallas_call(
        functools.partial(_resid_norm_kernel, slabs=slabs),
        grid=(S // tr,),
        in_specs=[row, row, vec, vec],
        out_specs=[x_spec, row],
        out_shape=[x_shape, jax.ShapeDtypeStruct((S, D), BF16)],
        compiler_params=_params("parallel"),
        name="resid_norm",
    )(y, x, g_post.reshape(1, D), g_pre.reshape(1, D))


def _mm_kernel(a_ref, w_ref, o_ref, *acc, nk):
    part = jnp.dot(a_ref[...], w_ref[...], preferred_element_type=F32)
    if nk == 1:
        o_ref[...] = part.astype(o_ref.dtype)
        return
    acc_ref, = acc
    k = pl.program_id(2)

    @pl.when(k == 0)
    def _():
        acc_ref[...] = part

    @pl.when(k > 0)
    def _():
        acc_ref[...] += part

    @pl.when(k == nk - 1)
    def _():
        o_ref[...] = acc_ref[...].astype(o_ref.dtype)


def matmul(a, w, out_dtype, *, tm=1024, tn=1024, tk=4096):
    M, K = a.shape
    N = w.shape[1]
    tm, tn, tk = _tile(M, tm, SUBLANES), _tile(N, tn), _tile(K, tk)
    nk = K // tk
    scratch = [pltpu.VMEM((tm, tn), F32)] if nk > 1 else []
    return pl.pallas_call(
        functools.partial(_mm_kernel, nk=nk),
        grid=(M // tm, N // tn, nk),
        in_specs=[pl.BlockSpec((tm, tk), lambda i, j, k: (i, k)),
                  pl.BlockSpec((tk, tn), lambda i, j, k: (k, j))],
        out_specs=pl.BlockSpec((tm, tn), lambda i, j, k: (i, j)),
        out_shape=jax.ShapeDtypeStruct((M, N), out_dtype),
        scratch_shapes=scratch,
        compiler_params=_params("parallel", "parallel", "arbitrary"),
        name="matmul",
    )(a, w)


def _rotate(x, c, s_lo, s_hi):
    half = ROT_DIM // 2
    return (x * c + pltpu.roll(x, LANES - half, 1) * s_lo + pltpu.roll(x, half, 1) * s_hi)


def _proj_rot_kernel(a_ref, w_ref, c_ref, slo_ref, shi_ref, o_ref, *, q_tiles, k_tiles):
    acc = jnp.dot(a_ref[...], w_ref[...], preferred_element_type=F32)
    j = pl.program_id(1)
    heads = acc.shape[1] // LANES

    def rotated(scale):
        c, s_lo, s_hi = c_ref[...], slo_ref[...], shi_ref[...]
        for g in range(heads):
            sl = slice(g * LANES, (g + 1) * LANES)
            o_ref[:, sl] = (_rotate(acc[:, sl], c, s_lo, s_hi) * scale).astype(o_ref.dtype)

    @pl.when(j < q_tiles)
    def _():
        rotated(ATTN_SCALE)

    @pl.when(jnp.logical_and(j >= q_tiles, j < q_tiles + k_tiles))
    def _():
        rotated(1.0)

    @pl.when(j >= q_tiles + k_tiles)
    def _():
        o_ref[...] = acc.astype(o_ref.dtype)


def proj_rotary(a, w, tables, qk_width, out_dtype, *, tm=1024, tn=1024):
    M, K = a.shape
    N = w.shape[1]
    tm, tn = _tile(M, tm, SUBLANES), _tile(qk_width, tn)
    assert N % tn == 0
    tab = pl.BlockSpec((tm, LANES), lambda i, j: (i, 0))
    return pl.pallas_call(
        functools.partial(_proj_rot_kernel, q_tiles=qk_width // tn, k_tiles=qk_width // tn),
        grid=(M // tm, N // tn),
        in_specs=[pl.BlockSpec((tm, K), lambda i, j: (i, 0)),
                  pl.BlockSpec((K, tn), lambda i, j: (0, j)),
                  tab, tab, tab],
        out_specs=pl.BlockSpec((tm, tn), lambda i, j: (i, j)),
        out_shape=jax.ShapeDtypeStruct((M, N), out_dtype),
        compiler_params=_params("parallel", "parallel"),
        name="proj_rotary",
    )(a, w, *tables)


def rotary_tables(seq):
    half = ROT_DIM // 2
    pos = jnp.arange(seq, dtype=F32)
    inv = ROPE_THETA ** (-jnp.arange(0, ROT_DIM, 2, dtype=F32) / ROT_DIM)
    ang = pos[:, None] * inv[None, :]
    cos, sin = jnp.cos(ang), jnp.sin(ang)
    zeros = jnp.zeros((seq, LANES - ROT_DIM), F32)
    z_half = jnp.zeros((seq, half), F32)
    c = jnp.concatenate([cos, cos, jnp.ones_like(zeros)], axis=1)
    s_lo = jnp.concatenate([-sin, z_half, zeros], axis=1)
    s_hi = jnp.concatenate([z_half, sin, zeros], axis=1)
    return c, s_lo, s_hi


def _gateup_kernel(a_ref, wg_ref, wu_ref, o_ref):
    a = a_ref[...]
    g = jnp.dot(a, wg_ref[...], preferred_element_type=F32)
    u = jnp.dot(a, wu_ref[...], preferred_element_type=F32)
    o_ref[...] = (g * jax.nn.sigmoid(g) * u).astype(o_ref.dtype)


def gateup(a, wg, wu, *, tm=1024, tn=512):
    M, K = a.shape
    N = wg.shape[1]
    tm, tn = _tile(M, tm, SUBLANES), _tile(N, tn)
    wspec = pl.BlockSpec((K, tn), lambda i, j: (0, j))
    return pl.pallas_call(
        _gateup_kernel,
        grid=(M // tm, N // tn),
        in_specs=[pl.BlockSpec((tm, K), lambda i, j: (i, 0)), wspec, wspec],
        out_specs=pl.BlockSpec((tm, tn), lambda i, j: (i, j)),
        out_shape=jax.ShapeDtypeStruct((M, N), BF16),
        compiler_params=_params("parallel", "parallel"),
        name="gateup",
    )(a, wg, wu)


def _dilated_kernel(q_ref, kp_ref, k_ref, vp_ref, v_ref, o_ref, oacc_ref, lse_ref):
    u = pl.program_id(0)
    unit = q_ref.shape[0]
    i = lax.broadcasted_iota(jnp.int32, (QBLK, 2 * QBLK), 0)
    j = lax.broadcasted_iota(jnp.int32, (QBLK, 2 * QBLK), 1)
    dist = QBLK + i - j
    has_prev_unit = u > 0

    for b, (window, r) in enumerate(A_BRANCHES):
        n_back = window // r
        band = jnp.logical_and(dist >= 0, dist <= n_back)
        first_band = jnp.logical_and(band, jnp.logical_or(j >= QBLK, has_prev_unit))
        span = r * QBLK
        for n in range(unit // span):
            for c in range(r):
                own = pl.ds(n * span + c, QBLK, stride=r)
                if n == 0:
                    prev = pl.ds(unit - span + c, QBLK, stride=r)
                    k_prev, v_prev, mask = kp_ref[prev, :], vp_ref[prev, :], first_band
                else:
                    prev = pl.ds((n - 1) * span + c, QBLK, stride=r)
                    k_prev, v_prev, mask = k_ref[prev, :], v_ref[prev, :], band
                q = q_ref[own, :].astype(BF16)
                kk = jnp.concatenate([k_prev, k_ref[own, :]], axis=0).astype(BF16)
                vv = jnp.concatenate([v_prev, v_ref[own, :]], axis=0).astype(BF16)
                s = lax.dot_general(q, kk, (((1,), (1,)), ((), ())), preferred_element_type=F32)
                s = jnp.where(mask, s, -jnp.inf)
                m = jnp.max(s, axis=-1, keepdims=True)
                p = jnp.exp(s - m)
                den = jnp.sum(p, axis=-1, keepdims=True)
                o = jnp.dot(p.astype(BF16), vv, preferred_element_type=F32) / den
                oacc_ref[b, own, :] = o
                lse_ref[b, own, :] = jnp.broadcast_to(m + jnp.log(den), (QBLK, LANES))

    lses = [lse_ref[b] for b in range(len(A_BRANCHES))]
    top = functools.reduce(jnp.maximum, lses)
    wts = [jnp.exp(l - top) for l in lses]
    num = sum(w * oacc_ref[b] for b, w in enumerate(wts))
    o_ref[...] = (num / sum(wts)).astype(o_ref.dtype)


def dilated_attention(proj, a_width):
    S = proj.shape[0]
    heads = a_width // HEAD_DIM
    unit = DIL_UNIT
    assert S % unit == 0, "sequence must be a multiple of the largest dilation unit"

    def cur(off):
        return pl.BlockSpec((unit, HEAD_DIM), lambda u, h: (u, off + h))

    def prev(off):
        return pl.BlockSpec((unit, HEAD_DIM), lambda u, h: (jnp.maximum(u - 1, 0), off + h))

    nb = len(A_BRANCHES)
    return pl.pallas_call(
        _dilated_kernel,
        grid=(S // unit, heads),
        in_specs=[cur(0), prev(heads), cur(heads), prev(2 * heads), cur(2 * heads)],
        out_specs=pl.BlockSpec((unit, HEAD_DIM), lambda u, h: (u, h)),
        out_shape=jax.ShapeDtypeStruct((S, a_width), BF16),
        scratch_shapes=[pltpu.VMEM((nb, unit, HEAD_DIM), F32),
                        pltpu.VMEM((nb, unit, LANES), F32)],
        compiler_params=_params("parallel", "parallel"),
        name="dilated_attention",
    )(proj, proj, proj, proj, proj)


def _gated_conv_kernel(gb_ref, gc_ref, hb_ref, gcp_ref, hbp_ref, w_ref, o_ref):
    u = gc_ref[...] * hb_ref[...]
    up = gcp_ref[...] * hbp_ref[...]
    up = jnp.where(pl.program_id(0) > 0, up, 0.0)
    row = lax.broadcasted_iota(jnp.int32, up.shape, 0)
    y = w_ref[CONV_WIDTH - 1:CONV_WIDTH, :] * u
    for back in range(1, CONV_WIDTH):
        shifted = pltpu.roll(u, back, 0)
        top = jnp.where(row < back, pltpu.roll(up, back, 0), shifted[:SUBLANES])
        shifted = jnp.concatenate([top, shifted[SUBLANES:]], axis=0)
        y = y + w_ref[CONV_WIDTH - 1 - back:CONV_WIDTH - back, :] * shifted
    o_ref[...] = (gb_ref[...] * y).astype(o_ref.dtype)


def gated_conv(proj, conv_w, col0, b_width, *, tr=512, tc=512):
    S = proj.shape[0]
    tr, tc = _tile(S, tr, SUBLANES), _tile(b_width, tc)
    assert col0 % tc == 0
    nbw = b_width // tc
    rb = tr // SUBLANES

    def cur(part):
        return pl.BlockSpec((tr, tc), lambda i, j: (i, col0 // tc + part * nbw + j))

    def prev(part):
        return pl.BlockSpec((SUBLANES, tc),
                            lambda i, j: (jnp.maximum(i * rb - 1, 0), col0 // tc + part * nbw + j))

    return pl.pallas_call(
        _gated_conv_kernel,
        grid=(S // tr, nbw),
        in_specs=[cur(0), cur(1), cur(2), prev(1), prev(2),
                  pl.BlockSpec((CONV_WIDTH, tc), lambda i, j: (0, j))],
        out_specs=pl.BlockSpec((tr, tc), lambda i, j: (i, j)),
        out_shape=jax.ShapeDtypeStruct((S, b_width), BF16),
        compiler_params=_params("parallel", "parallel"),
        name="gated_conv",
    )(proj, proj, proj, proj, proj, conv_w)


def _diff_attn_kernel(q_ref, k_ref, v_ref, lq1_ref, lk1_ref, lq2_ref, lk2_ref, g_ref, o_ref,
                      m_ref, l_ref, acc_ref, *, tq, lambda_init):
    qi = pl.program_id(1)
    m_ref[...] = jnp.full(m_ref.shape, -jnp.inf, F32)
    l_ref[...] = jnp.zeros(l_ref.shape, F32)
    acc_ref[...] = jnp.zeros(acc_ref.shape, F32)
    q = q_ref[...]
    qs = (q[:, :HEAD_DIM], q[:, HEAD_DIM:])

    def step(kb, masked):
        rows = pl.ds(pl.multiple_of(kb * tq, tq), tq)
        k = k_ref[rows, :]
        v = v_ref[rows, :]
        for c in range(2):
            kc = k[:, c * HEAD_DIM:(c + 1) * HEAD_DIM]
            s = lax.dot_general(qs[c], kc, (((1,), (1,)), ((), ())), preferred_element_type=F32)
            if masked:
                r_ = lax.broadcasted_iota(jnp.int32, s.shape, 0)
                c_ = lax.broadcasted_iota(jnp.int32, s.shape, 1)
                s = jnp.where(c_ <= r_, s, -jnp.inf)
            m_prev = m_ref[c]
            m_new = jnp.maximum(m_prev, jnp.max(s, axis=-1, keepdims=True))
            alpha = jnp.exp(m_prev - m_new)
            p = jnp.exp(s - m_new[:, :1])
            l_ref[c] = alpha * l_ref[c] + jnp.sum(p, axis=-1, keepdims=True)
            pv = jnp.dot(p.astype(BF16), v, preferred_element_type=F32)
            acc_ref[c] = acc_ref[c] * jnp.concatenate([alpha, alpha], axis=1) + pv
            m_ref[c] = m_new

    def body(kb, carry):
        step(kb, False)
        return carry

    lax.fori_loop(0, qi, body, 0)
    step(qi, True)

    f32 = lambda r: r[...].astype(F32)
    lam = (jnp.exp(jnp.sum(f32(lq1_ref) * f32(lk1_ref), axis=-1, keepdims=True))
           - jnp.exp(jnp.sum(f32(lq2_ref) * f32(lk2_ref), axis=-1, keepdims=True)) + lambda_init)
    inv1 = 1.0 / l_ref[0]
    inv2 = 1.0 / l_ref[1]
    o = (acc_ref[0] * jnp.concatenate([inv1, inv1], axis=1)
         - lam * (acc_ref[1] * jnp.concatenate([inv2, inv2], axis=1)))
    o_ref[...] = (_rms(o, g_ref[...]) * (1.0 - lambda_init)).astype(o_ref.dtype)


def diff_attention(qkv, lq1, lk1, lq2, lk2, subln_g, d_model, lambda_init, *, tq=512):
    S = qkv.shape[0]
    vd = 2 * HEAD_DIM
    heads = d_model // vd
    tq = _tile(S, tq, SUBLANES)
    vec = lambda n: pl.BlockSpec((1, n), lambda h, i: (0, 0))
    return pl.pallas_call(
        functools.partial(_diff_attn_kernel, tq=tq, lambda_init=lambda_init),
        grid=(heads, S // tq),
        in_specs=[pl.BlockSpec((tq, vd), lambda h, i: (i, h)),
                  pl.BlockSpec((S, vd), lambda h, i: (0, heads + h)),
                  pl.BlockSpec((S, vd), lambda h, i: (0, 2 * heads + h)),
                  vec(HEAD_DIM), vec(HEAD_DIM), vec(HEAD_DIM), vec(HEAD_DIM), vec(vd)],
        out_specs=pl.BlockSpec((tq, vd), lambda h, i: (i, h)),
        out_shape=jax.ShapeDtypeStruct((S, d_model), BF16),
        scratch_shapes=[pltpu.VMEM((2, tq, LANES), F32),
                        pltpu.VMEM((2, tq, LANES), F32),
                        pltpu.VMEM((2, tq, vd), F32)],
        compiler_params=_params("parallel", "arbitrary"),
        name="diff_attention",
    )(qkv, qkv, qkv, lq1.reshape(1, -1), lk1.reshape(1, -1), lq2.reshape(1, -1),
      lk2.reshape(1, -1), subln_g.reshape(1, -1))


def _router_kernel(h_ref, w_ref, gate_ref, idx_ref, *, n_experts):
    logits = jnp.dot(h_ref[...], w_ref[...], preferred_element_type=F32)
    lane = lax.broadcasted_iota(jnp.int32, logits.shape, 1)
    logits = jnp.where(lane < n_experts, logits, -jnp.inf)
    v1 = jnp.max(logits, axis=-1, keepdims=True)
    i1 = jnp.min(jnp.where(logits == v1, lane, LANES), axis=-1, keepdims=True)
    rest = jnp.where(lane == i1, -jnp.inf, logits)
    v2 = jnp.max(rest, axis=-1, keepdims=True)
    i2 = jnp.min(jnp.where(rest == v2, lane, LANES), axis=-1, keepdims=True)
    e2 = jnp.exp(v2 - v1)
    g1 = 1.0 / (1.0 + e2)
    gate_ref[...] = jnp.where(lane == 0, g1, jnp.where(lane == 1, e2 * g1, 0.0))
    idx_ref[...] = jnp.where(lane == 0, i1, jnp.where(lane == 1, i2, 0))


def router(h, router_w):
    S, D = h.shape
    n_experts = router_w.shape[1]
    assert n_experts <= LANES
    w = jnp.zeros((D, LANES), BF16).at[:, :n_experts].set(router_w.astype(BF16))
    tr = _tile(S, 512, SUBLANES)
    out = pl.BlockSpec((tr, LANES), lambda i: (i, 0))
    gates, idx = pl.pallas_call(
        functools.partial(_router_kernel, n_experts=n_experts),
        grid=(S // tr,),
        in_specs=[pl.BlockSpec((tr, D), lambda i: (i, 0)),
                  pl.BlockSpec((D, LANES), lambda i: (0, 0))],
        out_specs=[out, out],
        out_shape=[jax.ShapeDtypeStruct((S, LANES), F32),
                   jax.ShapeDtypeStruct((S, LANES), jnp.int32)],
        compiler_params=_params("parallel"),
        name="router",
    )(h, w)
    return gates[:, :TOP_K], idx[:, :TOP_K]


def dispatch_plan(idx, gates, n_experts, tm):
    S = idx.shape[0]
    pairs = S * TOP_K
    n_slots = pairs + n_experts * tm
    n_tiles = n_slots // tm
    e_flat = idx.reshape(pairs)
    onehot = (e_flat[:, None] == jnp.arange(n_experts, dtype=jnp.int32)[None, :]).astype(jnp.int32)
    rank = jnp.take_along_axis(jnp.cumsum(onehot, axis=0) - onehot, e_flat[:, None], axis=1)[:, 0]
    tiles_per_e = (jnp.sum(onehot, axis=0) + tm - 1) // tm
    tile_end = jnp.cumsum(tiles_per_e)
    slot0 = (tile_end - tiles_per_e) * tm
    dest = (slot0[e_flat] + rank).astype(jnp.int32)
    src_tok = jnp.zeros((n_slots,), jnp.int32).at[dest].set(jnp.arange(pairs, dtype=jnp.int32) // TOP_K)
    slot_gate = jnp.zeros((n_slots,), F32).at[dest].set(gates.reshape(pairs))
    tile_expert = jnp.minimum(
        jnp.searchsorted(tile_end, jnp.arange(n_tiles, dtype=jnp.int32), side="right"),
        n_experts - 1).astype(jnp.int32)
    n_valid = tile_end[-1:].astype(jnp.int32)
    return dest, src_tok, slot_gate.reshape(n_slots, 1), tile_expert, n_valid


def _row_slab(ref, row, chunks):
    return ref.at[pl.ds(pl.multiple_of(row * chunks, chunks), chunks), :]


def _grouped_gateup_kernel(tok_ref, te_ref, nv_ref, x_hbm, g_ref, wg_ref, wu_ref, o_ref,
                           buf_ref, a_ref, sem, *, tm):
    t = pl.program_id(0)
    valid = t < nv_ref[0]
    d = a_ref.shape[1]
    chunks = d // LANES

    @pl.when(jnp.logical_and(valid, pl.program_id(1) == 0))
    def _():
        base = t * tm

        def copy(r):
            return pltpu.make_async_copy(_row_slab(x_hbm, tok_ref[base + r], chunks),
                                         _row_slab(buf_ref, r, chunks), sem)

        def start(r, carry):
            copy(r).start()
            return carry

        def wait(r, carry):
            copy(r).wait()
            return carry

        lax.fori_loop(0, tm, start, 0)
        lax.fori_loop(0, tm, wait, 0)
        ssq = jnp.zeros((tm, 1), F32)
        for s in range(chunks):
            xs = _from_row_slabs(buf_ref, tm, s)
            ssq = ssq + jnp.sum(xs * xs, axis=-1, keepdims=True)
        rinv = lax.rsqrt(ssq / d + NORM_EPS)
        for s in range(chunks):
            sl = slice(s * LANES, (s + 1) * LANES)
            a_ref[:, sl] = (_from_row_slabs(buf_ref, tm, s) * rinv * g_ref[:, sl]).astype(a_ref.dtype)

    @pl.when(valid)
    def _():
        a = a_ref[...]
        g = jnp.dot(a, wg_ref[0], preferred_element_type=F32)
        u = jnp.dot(a, wu_ref[0], preferred_element_type=F32)
        o_ref[...] = (g * jax.nn.sigmoid(g) * u).astype(o_ref.dtype)

    @pl.when(jnp.logical_not(valid))
    def _():
        o_ref[...] = jnp.zeros(o_ref.shape, o_ref.dtype)


def grouped_gateup(x_slabs, g_pre, wg, wu, src_tok, tile_expert, n_valid, *, tm, tn=512):
    n_slots = src_tok.shape[0]
    K, N = wg.shape[1], wg.shape[2]
    chunks = K // LANES
    tn = _tile(N, tn)
    nj = N // tn

    def wmap(t, j, tok, te, nv):
        return (te[t], 0, jnp.where(t < nv[0], j, nj - 1))

    wspec = pl.BlockSpec((1, K, tn), wmap)
    return pl.pallas_call(
        functools.partial(_grouped_gateup_kernel, tm=tm),
        grid_spec=pltpu.PrefetchScalarGridSpec(
            num_scalar_prefetch=3,
            grid=(n_slots // tm, nj),
            in_specs=[pl.BlockSpec(memory_space=pl.ANY),
                      pl.BlockSpec((1, K), lambda t, j, tok, te, nv: (0, 0)),
                      wspec, wspec],
            out_specs=pl.BlockSpec((tm, tn), lambda t, j, tok, te, nv: (t, j)),
            scratch_shapes=[pltpu.VMEM((tm * chunks, LANES), F32),
                            pltpu.VMEM((tm, K), BF16),
                            pltpu.SemaphoreType.DMA(())]),
        out_shape=jax.ShapeDtypeStruct((n_slots, N), BF16),
        compiler_params=_params("arbitrary", "arbitrary"),
        name="grouped_gateup",
    )(src_tok, tile_expert, n_valid, x_slabs, g_pre.reshape(1, K), wg, wu)


def _grouped_down_kernel(te_ref, nv_ref, a_ref, w_ref, gate_ref, o_ref):
    t = pl.program_id(0)

    @pl.when(t < nv_ref[0])
    def _():
        y = gate_ref[...] * jnp.dot(a_ref[...], w_ref[0], preferred_element_type=F32)
        for s in range(o_ref.shape[1]):
            o_ref[:, s, :] = y[:, s * LANES:(s + 1) * LANES]

    @pl.when(t >= nv_ref[0])
    def _():
        o_ref[...] = jnp.zeros(o_ref.shape, o_ref.dtype)


def grouped_down(act, wd, slot_gate, tile_expert, n_valid, *, tm, tn=1024):
    n_slots, K = act.shape
    N = wd.shape[2]
    tn = _tile(N, tn)
    nj = N // tn

    def wmap(t, j, te, nv):
        return (te[t], 0, jnp.where(t < nv[0], j, nj - 1))

    return pl.pallas_call(
        _grouped_down_kernel,
        grid_spec=pltpu.PrefetchScalarGridSpec(
            num_scalar_prefetch=2,
            grid=(n_slots // tm, nj),
            in_specs=[pl.BlockSpec((tm, K), lambda t, j, te, nv: (t, 0)),
                      pl.BlockSpec((1, K, tn), wmap),
                      pl.BlockSpec((tm, 1), lambda t, j, te, nv: (t, 0))],
            out_specs=pl.BlockSpec((tm, tn // LANES, LANES), lambda t, j, te, nv: (t, j, 0))),
        out_shape=jax.ShapeDtypeStruct((n_slots, N // LANES, LANES), F32),
        compiler_params=_params("arbitrary", "arbitrary"),
        name="grouped_down",
    )(tile_expert, n_valid, act, wd, slot_gate)


def _combine_kernel(dest_ref, y_hbm, x_ref, g_ref, o_ref, buf_ref, sem, *, tr):
    base = pl.program_id(0) * tr
    d = o_ref.shape[1]
    chunks = d // LANES

    def copy(r, k):
        return pltpu.make_async_copy(_row_slab(y_hbm, dest_ref[(base + r) * TOP_K + k], chunks),
                                     _row_slab(buf_ref.at[k], r, chunks), sem)

    def start(r, carry):
        for k in range(TOP_K):
            copy(r, k).start()
        return carry

    def wait(r, carry):
        for k in range(TOP_K):
            copy(r, k).wait()
        return carry

    lax.fori_loop(0, tr, start, 0)
    lax.fori_loop(0, tr, wait, 0)

    def y_chunk(s):
        return sum(_from_row_slabs(buf_ref.at[k], tr, s) for k in range(TOP_K))

    ssq = jnp.zeros((tr, 1), F32)
    for s in range(chunks):
        ys = y_chunk(s)
        ssq = ssq + jnp.sum(ys * ys, axis=-1, keepdims=True)
    rinv = lax.rsqrt(ssq / d + NORM_EPS)
    for s in range(chunks):
        sl = slice(s * LANES, (s + 1) * LANES)
        o_ref[:, sl] = _from_row_slabs(x_ref, tr, s) + y_chunk(s) * rinv * g_ref[:, sl]


def combine(y_slabs, dest, x_slabs, g_post, *, tr=256):
    D = g_post.shape[0]
    chunks = D // LANES
    S = x_slabs.shape[0] // chunks
    tr = _tile(S, tr, SUBLANES)
    return pl.pallas_call(
        functools.partial(_combine_kernel, tr=tr),
        grid_spec=pltpu.PrefetchScalarGridSpec(
            num_scalar_prefetch=1,
            grid=(S // tr,),
            in_specs=[pl.BlockSpec(memory_space=pl.ANY),
                      pl.BlockSpec((tr * chunks, LANES), lambda i, d: (i, 0)),
                      pl.BlockSpec((1, D), lambda i, d: (0, 0))],
            out_specs=pl.BlockSpec((tr, D), lambda i, d: (i, 0)),
            scratch_shapes=[pltpu.VMEM((TOP_K, tr * chunks, LANES), F32),
                            pltpu.SemaphoreType.DMA(())]),
        out_shape=jax.ShapeDtypeStruct((S, D), F32),
        compiler_params=_params("arbitrary"),
        name="combine",
    )(dest, y_slabs, x_slabs, g_post.reshape(1, D))


def moe_ffn(x_slabs, h, g_pre, router_w, wg, wu, wd, g_post, *, tm):
    n_experts = router_w.shape[1]
    gates, idx = router(h, router_w)
    dest, src_tok, slot_gate, tile_expert, n_valid = dispatch_plan(idx, gates, n_experts, tm)
    act = grouped_gateup(x_slabs, g_pre, wg, wu, src_tok, tile_expert, n_valid, tm=tm)
    y = grouped_down(act, wd, slot_gate, tile_expert, n_valid, tm=tm)
    return combine(y.reshape(-1, LANES), dest, x_slabs, g_post)


def _pad_axis(w, axis, mult):
    pad = (-w.shape[axis]) % mult
    if pad == 0:
        return w
    widths = [(0, 0)] * w.ndim
    widths[axis] = (0, pad)
    return jnp.pad(w, widths)


def kernel(x, l0_norm_pre_mix, l0_w_in, l0_conv_w, l0_w_out, l0_norm_post_mix, l0_norm_pre_ffn, l0_ffn_w_gate, l0_ffn_w_up, l0_ffn_w_down, l0_norm_post_ffn, l1_norm_pre_mix, l1_w_qkv, l1_lambda_q1, l1_lambda_k1, l1_lambda_q2, l1_lambda_k2, l1_subln_g, l1_w_out, l1_norm_post_mix, l1_norm_pre_ffn, l1_router_w, l1_moe_w_gate, l1_moe_w_up, l1_moe_w_down, l1_norm_post_ffn):
    B, S, D = x.shape
    assert B == 1, "attention kernels handle a single sequence"
    a_width = D // 2
    b_width = D - a_width
    n_experts = l1_router_w.shape[1]
    tables = rotary_tables(S)
    xr = x.reshape(S, D)

    h = rmsnorm_cast(xr, l0_norm_pre_mix)
    proj = proj_rotary(h, l0_w_in.astype(BF16), tables, a_width, F32)
    a_out = dilated_attention(proj, a_width)
    b_out = gated_conv(proj, l0_conv_w, 3 * a_width, b_width)
    mix = matmul(jnp.concatenate([a_out, b_out], axis=1), l0_w_out.astype(BF16), F32)
    xr, h = resid_norm(mix, xr, l0_norm_post_mix, l0_norm_pre_ffn)
    ff_tile = 1024
    wg = _pad_axis(l0_ffn_w_gate.astype(BF16), 1, ff_tile)
    wu = _pad_axis(l0_ffn_w_up.astype(BF16), 1, ff_tile)
    wd = _pad_axis(l0_ffn_w_down.astype(BF16), 0, ff_tile)
    act = gateup(h, wg, wu)
    ff = matmul(act, wd, F32, tk=_tile(wd.shape[0], 2816))
    xr, h = resid_norm(ff, xr, l0_norm_post_ffn, l1_norm_pre_mix)

    lambda_init = 0.8 - 0.6 * math.exp(-0.3 * 1)
    qkv = proj_rotary(h, l1_w_qkv.astype(BF16), tables, D, BF16)
    o = diff_attention(qkv, l1_lambda_q1, l1_lambda_k1, l1_lambda_q2, l1_lambda_k2,
                       l1_subln_g, D, lambda_init)
    mix = matmul(o, l1_w_out.astype(BF16), F32)
    x_slabs, h = resid_norm(mix, xr, l1_norm_post_mix, l1_norm_pre_ffn, slabs=True)
    out = moe_ffn(x_slabs, h, l1_norm_pre_ffn, l1_router_w, l1_moe_w_gate.astype(BF16),
                  l1_moe_w_up.astype(BF16), l1_moe_w_down.astype(BF16), l1_norm_post_ffn,
                  tm=_tile(S, 512, SUBLANES))
    return out.reshape(B, S, D)
```

```python
import functools
import math

import jax
import jax.numpy as jnp
from jax import lax
from jax.experimental import pallas as pl
from jax.experimental.pallas import tpu as pltpu

HEAD_DIM = 128
ROT_DIM = HEAD_DIM // 4
ROPE_THETA = 500000.0
QBLK = 128
A_BRANCHES = ((128, 1), (512, 4), (2048, 16))
DIL_UNIT = max(r for _, r in A_BRANCHES) * QBLK
CONV_WIDTH = 3
TOP_K = 2
NORM_EPS = 1e-6
ATTN_SCALE = HEAD_DIM ** -0.5

LANES = 128
SUBLANES = 8
VMEM_BYTES_V7X = 64 * 1024 * 1024
VMEM_LIMIT = VMEM_BYTES_V7X - 8 * 1024 * 1024

F32 = jnp.float32
BF16 = jnp.bfloat16


def _tile(n, pref, align=LANES):
    if n <= pref:
        return n
    t = (pref // align) * align
    while t >= align:
        if n % t == 0:
            return t
        t -= align
    raise ValueError(f"no {align}-aligned tile <= {pref} divides {n}")


def _params(*sem):
    return pltpu.CompilerParams(dimension_semantics=sem, vmem_limit_bytes=VMEM_LIMIT)


def _rms(x, g):
    ms = jnp.mean(x * x, axis=-1, keepdims=True)
    return x * lax.rsqrt(ms + NORM_EPS) * g


def _rmsnorm_kernel(x_ref, g_ref, h_ref):
    h_ref[...] = _rms(x_ref[...], g_ref[...]).astype(h_ref.dtype)


def rmsnorm_cast(x, g):
    S, D = x.shape
    tr = _tile(S, 256, SUBLANES)
    return pl.pallas_call(
        _rmsnorm_kernel,
        grid=(S // tr,),
        in_specs=[pl.BlockSpec((tr, D), lambda i: (i, 0)),
                  pl.BlockSpec((1, D), lambda i: (0, 0))],
        out_specs=pl.BlockSpec((tr, D), lambda i: (i, 0)),
        out_shape=jax.ShapeDtypeStruct((S, D), BF16),
        compiler_params=_params("parallel"),
        name="rmsnorm_cast",
    )(x, g.reshape(1, D))


def _slab_pitch(chunks):
    if chunks % SUBLANES or (chunks // SUBLANES) % 2:
        return chunks
    return chunks + SUBLANES


def _to_row_slabs(ref, val, pitch):
    rows, d = val.shape
    chunks = d // LANES
    for s in range(pitch):
        part = val[:, s * LANES:(s + 1) * LANES] if s < chunks else jnp.zeros((rows, LANES), val.dtype)
        ref[pl.ds(s, rows, stride=pitch), :] = part


def _from_row_slabs(ref, rows, s, pitch):
    return ref[pl.ds(s, rows, stride=pitch), :]


def _resid_norm_kernel(y_ref, x_ref, gpost_ref, gpre_ref, xo_ref, h_ref, *slab_ref, pitch):
    xn = x_ref[...] + _rms(y_ref[...], gpost_ref[...])
    xo_ref[...] = xn
    h_ref[...] = _rms(xn, gpre_ref[...]).astype(h_ref.dtype)
    if slab_ref:
        _to_row_slabs(slab_ref[0], xn, pitch)


def resid_norm(y, x, g_post, g_pre, *, slabs=False):
    S, D = x.shape
    tr = _tile(S, 256, SUBLANES)
    row = pl.BlockSpec((tr, D), lambda i: (i, 0))
    vec = pl.BlockSpec((1, D), lambda i: (0, 0))
    pitch = _slab_pitch(D // LANES)
    out_specs = [row, row]
    out_shape = [jax.ShapeDtypeStruct((S, D), F32), jax.ShapeDtypeStruct((S, D), BF16)]
    if slabs:
        out_specs.append(pl.BlockSpec((tr * pitch, LANES), lambda i: (i, 0)))
        out_shape.append(jax.ShapeDtypeStruct((S * pitch, LANES), F32))
    return pl.pallas_call(
        functools.partial(_resid_norm_kernel, pitch=pitch),
        grid=(S // tr,),
        in_specs=[row, row, vec, vec],
        out_specs=out_specs,
        out_shape=out_shape,
        compiler_params=_params("parallel"),
        name="resid_norm",
    )(y, x, g_post.reshape(1, D), g_pre.reshape(1, D))


def _mm_kernel(a_ref, w_ref, o_ref, *acc, nk, lhs_transposed):
    contract = 0 if lhs_transposed else 1
    part = lax.dot_general(a_ref[...], w_ref[...], (((contract,), (0,)), ((), ())),
                           preferred_element_type=F32)
    if nk == 1:
        o_ref[...] = part.astype(o_ref.dtype)
        return
    acc_ref, = acc
    k = pl.program_id(2)

    @pl.when(k == 0)
    def _():
        acc_ref[...] = part

    @pl.when(k > 0)
    def _():
        acc_ref[...] += part

    @pl.when(k == nk - 1)
    def _():
        o_ref[...] = acc_ref[...].astype(o_ref.dtype)


def matmul(a, w, out_dtype, *, tm=1024, tn=1024, tk=4096, lhs_transposed=False):
    M, K = a.shape[::-1] if lhs_transposed else a.shape
    N = w.shape[1]
    tm, tn, tk = _tile(M, tm, LANES if lhs_transposed else SUBLANES), _tile(N, tn), _tile(K, tk)
    nk = K // tk
    scratch = [pltpu.VMEM((tm, tn), F32)] if nk > 1 else []
    if lhs_transposed:
        a_spec = pl.BlockSpec((tk, tm), lambda i, j, k: (k, i))
    else:
        a_spec = pl.BlockSpec((tm, tk), lambda i, j, k: (i, k))
    return pl.pallas_call(
        functools.partial(_mm_kernel, nk=nk, lhs_transposed=lhs_transposed),
        grid=(M // tm, N // tn, nk),
        in_specs=[a_spec,
                  pl.BlockSpec((tk, tn), lambda i, j, k: (k, j))],
        out_specs=pl.BlockSpec((tm, tn), lambda i, j, k: (i, j)),
        out_shape=jax.ShapeDtypeStruct((M, N), out_dtype),
        scratch_shapes=scratch,
        compiler_params=_params("parallel", "parallel", "arbitrary"),
        name="matmul",
    )(a, w)


def _rotate(x, c, s_lo, s_hi):
    half = ROT_DIM // 2
    return (x * c + pltpu.roll(x, LANES - half, 1) * s_lo + pltpu.roll(x, half, 1) * s_hi)


def _proj_rot_kernel(a_ref, w_ref, c_ref, slo_ref, shi_ref, o_ref, *, q_tiles, k_tiles):
    acc = jnp.dot(a_ref[...], w_ref[...], preferred_element_type=F32)
    j = pl.program_id(1)
    heads = acc.shape[1] // LANES

    def rotated(scale):
        c, s_lo, s_hi = c_ref[...], slo_ref[...], shi_ref[...]
        for g in range(heads):
            sl = slice(g * LANES, (g + 1) * LANES)
            o_ref[:, sl] = (_rotate(acc[:, sl], c, s_lo, s_hi) * scale).astype(o_ref.dtype)

    @pl.when(j < q_tiles)
    def _():
        rotated(ATTN_SCALE)

    @pl.when(jnp.logical_and(j >= q_tiles, j < q_tiles + k_tiles))
    def _():
        rotated(1.0)

    @pl.when(j >= q_tiles + k_tiles)
    def _():
        o_ref[...] = acc.astype(o_ref.dtype)


def proj_rotary(a, w, tables, qk_width, out_dtype, *, tm=1024, tn=1024):
    M, K = a.shape
    N = w.shape[1]
    tm, tn = _tile(M, tm, SUBLANES), _tile(qk_width, tn)
    assert N % tn == 0
    tab = pl.BlockSpec((tm, LANES), lambda i, j: (i, 0))
    return pl.pallas_call(
        functools.partial(_proj_rot_kernel, q_tiles=qk_width // tn, k_tiles=qk_width // tn),
        grid=(M // tm, N // tn),
        in_specs=[pl.BlockSpec((tm, K), lambda i, j: (i, 0)),
                  pl.BlockSpec((K, tn), lambda i, j: (0, j)),
                  tab, tab, tab],
        out_specs=pl.BlockSpec((tm, tn), lambda i, j: (i, j)),
        out_shape=jax.ShapeDtypeStruct((M, N), out_dtype),
        compiler_params=_params("parallel", "parallel"),
        name="proj_rotary",
    )(a, w, *tables)


def rotary_tables(seq):
    half = ROT_DIM // 2
    pos = jnp.arange(seq, dtype=F32)
    inv = ROPE_THETA ** (-jnp.arange(0, ROT_DIM, 2, dtype=F32) / ROT_DIM)
    ang = pos[:, None] * inv[None, :]
    cos, sin = jnp.cos(ang), jnp.sin(ang)
    zeros = jnp.zeros((seq, LANES - ROT_DIM), F32)
    z_half = jnp.zeros((seq, half), F32)
    c = jnp.concatenate([cos, cos, jnp.ones_like(zeros)], axis=1)
    s_lo = jnp.concatenate([-sin, z_half, zeros], axis=1)
    s_hi = jnp.concatenate([z_half, sin, zeros], axis=1)
    return c, s_lo, s_hi


def _gateup_kernel(a_ref, wg_ref, wu_ref, o_ref):
    a = a_ref[...]
    g = jnp.dot(a, wg_ref[...], preferred_element_type=F32)
    u = jnp.dot(a, wu_ref[...], preferred_element_type=F32)
    o_ref[...] = (g * jax.nn.sigmoid(g) * u).astype(o_ref.dtype)


def gateup(a, wg, wu, *, tm=1024, tn=512):
    M, K = a.shape
    N = wg.shape[1]
    tm, tn = _tile(M, tm, SUBLANES), _tile(N, tn)
    wspec = pl.BlockSpec((K, tn), lambda i, j: (0, j))
    return pl.pallas_call(
        _gateup_kernel,
        grid=(M // tm, N // tn),
        in_specs=[pl.BlockSpec((tm, K), lambda i, j: (i, 0)), wspec, wspec],
        out_specs=pl.BlockSpec((tm, tn), lambda i, j: (i, j)),
        out_shape=jax.ShapeDtypeStruct((M, N), BF16),
        compiler_params=_params("parallel", "parallel"),
        name="gateup",
    )(a, wg, wu)


def _dilated_kernel(q_ref, kp_ref, k_ref, vp_ref, v_ref, o_ref, oacc_ref, lse_ref):
    u = pl.program_id(0)
    unit = q_ref.shape[0]
    i = lax.broadcasted_iota(jnp.int32, (QBLK, 2 * QBLK), 0)
    j = lax.broadcasted_iota(jnp.int32, (QBLK, 2 * QBLK), 1)
    dist = QBLK + i - j
    has_prev_unit = u > 0

    for b, (window, r) in enumerate(A_BRANCHES):
        n_back = window // r
        band = jnp.logical_and(dist >= 0, dist <= n_back)
        first_band = jnp.logical_and(band, jnp.logical_or(j >= QBLK, has_prev_unit))
        span = r * QBLK
        for n in range(unit // span):
            for c in range(r):
                own = pl.ds(n * span + c, QBLK, stride=r)
                if n == 0:
                    prev = pl.ds(unit - span + c, QBLK, stride=r)
                    k_prev, v_prev, mask = kp_ref[prev, :], vp_ref[prev, :], first_band
                else:
                    prev = pl.ds((n - 1) * span + c, QBLK, stride=r)
                    k_prev, v_prev, mask = k_ref[prev, :], v_ref[prev, :], band
                q = q_ref[own, :].astype(BF16)
                kk = jnp.concatenate([k_prev, k_ref[own, :]], axis=0).astype(BF16)
                vv = jnp.concatenate([v_prev, v_ref[own, :]], axis=0).astype(BF16)
                s = lax.dot_general(q, kk, (((1,), (1,)), ((), ())), preferred_element_type=F32)
                s = jnp.where(mask, s, -jnp.inf)
                m = jnp.max(s, axis=-1, keepdims=True)
                p = jnp.exp(s - m)
                den = jnp.sum(p, axis=-1, keepdims=True)
                o = jnp.dot(p.astype(BF16), vv, preferred_element_type=F32) / den
                oacc_ref[b, own, :] = o
                lse_ref[b, own, :] = jnp.broadcast_to(m + jnp.log(den), (QBLK, LANES))

    lses = [lse_ref[b] for b in range(len(A_BRANCHES))]
    top = functools.reduce(jnp.maximum, lses)
    wts = [jnp.exp(l - top) for l in lses]
    num = sum(w * oacc_ref[b] for b, w in enumerate(wts))
    o_ref[...] = (num / sum(wts)).astype(o_ref.dtype)


def dilated_attention(proj, a_width):
    S = proj.shape[0]
    heads = a_width // HEAD_DIM
    unit = DIL_UNIT
    assert S % unit == 0, "sequence must be a multiple of the largest dilation unit"

    def cur(off):
        return pl.BlockSpec((unit, HEAD_DIM), lambda u, h: (u, off + h))

    def prev(off):
        return pl.BlockSpec((unit, HEAD_DIM), lambda u, h: (jnp.maximum(u - 1, 0), off + h))

    nb = len(A_BRANCHES)
    return pl.pallas_call(
        _dilated_kernel,
        grid=(S // unit, heads),
        in_specs=[cur(0), prev(heads), cur(heads), prev(2 * heads), cur(2 * heads)],
        out_specs=pl.BlockSpec((unit, HEAD_DIM), lambda u, h: (u, h)),
        out_shape=jax.ShapeDtypeStruct((S, a_width), BF16),
        scratch_shapes=[pltpu.VMEM((nb, unit, HEAD_DIM), F32),
                        pltpu.VMEM((nb, unit, LANES), F32)],
        compiler_params=_params("parallel", "parallel"),
        name="dilated_attention",
    )(proj, proj, proj, proj, proj)


def _gated_conv_kernel(gb_ref, gc_ref, hb_ref, gcp_ref, hbp_ref, w_ref, o_ref):
    u = gc_ref[...] * hb_ref[...]
    up = gcp_ref[...] * hbp_ref[...]
    up = jnp.where(pl.program_id(0) > 0, up, 0.0)
    row = lax.broadcasted_iota(jnp.int32, up.shape, 0)
    y = w_ref[CONV_WIDTH - 1:CONV_WIDTH, :] * u
    for back in range(1, CONV_WIDTH):
        shifted = pltpu.roll(u, back, 0)
        top = jnp.where(row < back, pltpu.roll(up, back, 0), shifted[:SUBLANES])
        shifted = jnp.concatenate([top, shifted[SUBLANES:]], axis=0)
        y = y + w_ref[CONV_WIDTH - 1 - back:CONV_WIDTH - back, :] * shifted
    o_ref[...] = (gb_ref[...] * y).astype(o_ref.dtype)


def gated_conv(proj, conv_w, col0, b_width, *, tr=512, tc=512):
    S = proj.shape[0]
    tr, tc = _tile(S, tr, SUBLANES), _tile(b_width, tc)
    assert col0 % tc == 0
    nbw = b_width // tc
    rb = tr // SUBLANES

    def cur(part):
        return pl.BlockSpec((tr, tc), lambda i, j: (i, col0 // tc + part * nbw + j))

    def prev(part):
        return pl.BlockSpec((SUBLANES, tc),
                            lambda i, j: (jnp.maximum(i * rb - 1, 0), col0 // tc + part * nbw + j))

    return pl.pallas_call(
        _gated_conv_kernel,
        grid=(S // tr, nbw),
        in_specs=[cur(0), cur(1), cur(2), prev(1), prev(2),
                  pl.BlockSpec((CONV_WIDTH, tc), lambda i, j: (0, j))],
        out_specs=pl.BlockSpec((tr, tc), lambda i, j: (i, j)),
        out_shape=jax.ShapeDtypeStruct((S, b_width), BF16),
        compiler_params=_params("parallel", "parallel"),
        name="gated_conv",
    )(proj, proj, proj, proj, proj, conv_w)


def _diff_attn_kernel(q_ref, k_ref, vt_ref, lq1_ref, lk1_ref, lq2_ref, lk2_ref, g_ref, o_ref,
                      m_ref, l_ref, acc_ref, sa_ref, sb_ref, *, tq, lambda_init):
    qi = pl.program_id(1)
    m_ref[...] = jnp.full(m_ref.shape, -jnp.inf, F32)
    l_ref[...] = jnp.zeros(l_ref.shape, F32)
    acc_ref[...] = jnp.zeros(acc_ref.shape, F32)
    q = q_ref[...]
    qs = (q[:, :HEAD_DIM], q[:, HEAD_DIM:])

    def scores(kb, s_ref):
        k = k_ref[pl.ds(pl.multiple_of(kb * tq, tq), tq), :]
        for c in range(2):
            s_ref[c] = lax.dot_general(k[:, c * HEAD_DIM:(c + 1) * HEAD_DIM], qs[c],
                                       (((1,), (1,)), ((), ())), preferred_element_type=F32)

    def consume(kb, s_ref, diagonal):
        vt = vt_ref[0, kb]
        for c in range(2):
            s = s_ref[c]
            if diagonal:
                key = lax.broadcasted_iota(jnp.int32, s.shape, 0)
                qry = lax.broadcasted_iota(jnp.int32, s.shape, 1)
                s = jnp.where(key <= qry, s, -jnp.inf)
            m_prev = m_ref[c]
            m_new = jnp.maximum(m_prev, jnp.max(s, axis=0, keepdims=True))
            alpha = jnp.exp(m_prev - m_new)
            p = jnp.exp(s - m_new)
            l_ref[c] = alpha * l_ref[c] + jnp.sum(p, axis=0, keepdims=True)
            pv = jnp.dot(vt, p.astype(BF16), preferred_element_type=F32)
            acc_ref[c] = acc_ref[c] * alpha + pv
            m_ref[c] = m_new

    scores(0, sa_ref)

    def body(j, carry):
        scores(2 * j + 1, sb_ref)
        consume(2 * j, sa_ref, False)
        scores(2 * j + 2, sa_ref)
        consume(2 * j + 1, sb_ref, False)
        return carry

    lax.fori_loop(0, qi // 2, body, 0)

    @pl.when(qi % 2 == 1)
    def _():
        scores(qi, sb_ref)
        consume(qi - 1, sa_ref, False)
        consume(qi, sb_ref, True)

    @pl.when(qi % 2 == 0)
    def _():
        consume(qi, sa_ref, True)

    f32 = lambda r: r[...].astype(F32)
    lam = (jnp.exp(jnp.sum(f32(lq1_ref) * f32(lk1_ref), axis=-1, keepdims=True))
           - jnp.exp(jnp.sum(f32(lq2_ref) * f32(lk2_ref), axis=-1, keepdims=True)) + lambda_init)
    o = acc_ref[0] * (1.0 / l_ref[0]) - lam * (acc_ref[1] * (1.0 / l_ref[1]))
    ms = jnp.mean(o * o, axis=0, keepdims=True)
    o_ref[...] = (o * lax.rsqrt(ms + NORM_EPS) * g_ref[...] * (1.0 - lambda_init)).astype(o_ref.dtype)


def diff_attention(qkv, lq1, lk1, lq2, lk2, subln_g, d_model, lambda_init, *, tq=512):
    S = qkv.shape[0]
    vd = 2 * HEAD_DIM
    heads = d_model // vd
    tq = _tile(S, tq)
    nkb = S // tq
    vt = qkv[:, 2 * d_model:].reshape(nkb, tq, heads, vd).transpose(2, 0, 3, 1)
    vec = lambda n: pl.BlockSpec((1, n), lambda h, i: (0, 0))
    return pl.pallas_call(
        functools.partial(_diff_attn_kernel, tq=tq, lambda_init=lambda_init),
        grid=(heads, S // tq),
        in_specs=[pl.BlockSpec((tq, vd), lambda h, i: (i, h)),
                  pl.BlockSpec((S, vd), lambda h, i: (0, heads + h)),
                  pl.BlockSpec((1, nkb, vd, tq), lambda h, i: (h, 0, 0, 0)),
                  vec(HEAD_DIM), vec(HEAD_DIM), vec(HEAD_DIM), vec(HEAD_DIM),
                  pl.BlockSpec((vd, 1), lambda h, i: (0, 0))],
        out_specs=pl.BlockSpec((vd, tq), lambda h, i: (h, i)),
        out_shape=jax.ShapeDtypeStruct((d_model, S), BF16),
        scratch_shapes=[pltpu.VMEM((2, 1, tq), F32),
                        pltpu.VMEM((2, 1, tq), F32),
                        pltpu.VMEM((2, vd, tq), F32),
                        pltpu.VMEM((2, tq, tq), F32),
                        pltpu.VMEM((2, tq, tq), F32)],
        compiler_params=_params("parallel", "arbitrary"),
        name="diff_attention",
    )(qkv, qkv, vt, lq1.reshape(1, -1), lk1.reshape(1, -1), lq2.reshape(1, -1),
      lk2.reshape(1, -1), subln_g.reshape(-1, 1))


def _router_kernel(h_ref, w_ref, gate_ref, idx_ref, *, n_experts):
    logits = jnp.dot(h_ref[...], w_ref[...], preferred_element_type=F32)
    lane = lax.broadcasted_iota(jnp.int32, logits.shape, 1)
    logits = jnp.where(lane < n_experts, logits, -jnp.inf)
    v1 = jnp.max(logits, axis=-1, keepdims=True)
    i1 = jnp.min(jnp.where(logits == v1, lane, LANES), axis=-1, keepdims=True)
    rest = jnp.where(lane == i1, -jnp.inf, logits)
    v2 = jnp.max(rest, axis=-1, keepdims=True)
    i2 = jnp.min(jnp.where(rest == v2, lane, LANES), axis=-1, keepdims=True)
    e2 = jnp.exp(v2 - v1)
    g1 = 1.0 / (1.0 + e2)
    gate_ref[...] = jnp.where(lane == 0, g1, jnp.where(lane == 1, e2 * g1, 0.0))
    idx_ref[...] = jnp.where(lane == 0, i1, jnp.where(lane == 1, i2, 0))


def router(h, router_w):
    S, D = h.shape
    n_experts = router_w.shape[1]
    assert n_experts <= LANES
    w = jnp.zeros((D, LANES), BF16).at[:, :n_experts].set(router_w.astype(BF16))
    tr = _tile(S, 512, SUBLANES)
    out = pl.BlockSpec((tr, LANES), lambda i: (i, 0))
    gates, idx = pl.pallas_call(
        functools.partial(_router_kernel, n_experts=n_experts),
        grid=(S // tr,),
        in_specs=[pl.BlockSpec((tr, D), lambda i: (i, 0)),
                  pl.BlockSpec((D, LANES), lambda i: (0, 0))],
        out_specs=[out, out],
        out_shape=[jax.ShapeDtypeStruct((S, LANES), F32),
                   jax.ShapeDtypeStruct((S, LANES), jnp.int32)],
        compiler_params=_params("parallel"),
        name="router",
    )(h, w)
    return gates[:, :TOP_K], idx[:, :TOP_K]


def dispatch_plan(idx, gates, n_experts, tm):
    S = idx.shape[0]
    pairs = S * TOP_K
    n_slots = pairs + n_experts * tm
    n_tiles = n_slots // tm
    e_flat = idx.reshape(pairs)
    onehot = (e_flat[:, None] == jnp.arange(n_experts, dtype=jnp.int32)[None, :]).astype(jnp.int32)
    rank = jnp.take_along_axis(jnp.cumsum(onehot, axis=0) - onehot, e_flat[:, None], axis=1)[:, 0]
    tiles_per_e = (jnp.sum(onehot, axis=0) + tm - 1) // tm
    tile_end = jnp.cumsum(tiles_per_e)
    slot0 = (tile_end - tiles_per_e) * tm
    dest = (slot0[e_flat] + rank).astype(jnp.int32)
    src_tok = jnp.zeros((n_slots,), jnp.int32).at[dest].set(jnp.arange(pairs, dtype=jnp.int32) // TOP_K)
    slot_gate = jnp.zeros((n_slots,), F32).at[dest].set(gates.reshape(pairs))
    tile_ids = jnp.arange(n_tiles, dtype=jnp.int32)
    tile_expert = jnp.minimum(jnp.sum((tile_end[None, :] <= tile_ids[:, None]).astype(jnp.int32), axis=1),
                              n_experts - 1).astype(jnp.int32)
    n_valid = tile_end[-1:].astype(jnp.int32)
    return dest, src_tok, slot_gate.reshape(n_slots, 1), tile_expert, n_valid


def _row_slab(ref, row, chunks, pitch):
    align = math.gcd(pitch, SUBLANES) if pitch % SUBLANES else SUBLANES
    return ref.at[pl.ds(pl.multiple_of(row * pitch, align), chunks), :]


def _grouped_gateup_kernel(tok_ref, te_ref, nv_ref, x_hbm, g_ref, wg_ref, wu_ref, o_ref,
                           buf_ref, a_ref, sem, *, tm):
    t = pl.program_id(0)
    valid = t < nv_ref[0]
    d = a_ref.shape[1]
    chunks = d // LANES
    pitch = buf_ref.shape[0] // tm

    @pl.when(jnp.logical_and(valid, pl.program_id(1) == 0))
    def _():
        base = t * tm

        def copy(r):
            return pltpu.make_async_copy(_row_slab(x_hbm, tok_ref[base + r], chunks, pitch),
                                         _row_slab(buf_ref, r, chunks, pitch), sem)

        def start(r, carry):
            copy(r).start()
            return carry

        def wait(r, carry):
            copy(r).wait()
            return carry

        lax.fori_loop(0, tm, start, 0)
        lax.fori_loop(0, tm, wait, 0)
        ssq = jnp.zeros((tm, 1), F32)
        for s in range(chunks):
            xs = _from_row_slabs(buf_ref, tm, s, pitch)
            ssq = ssq + jnp.sum(xs * xs, axis=-1, keepdims=True)
        rinv = lax.rsqrt(ssq / d + NORM_EPS)
        for s in range(chunks):
            sl = slice(s * LANES, (s + 1) * LANES)
            xs = _from_row_slabs(buf_ref, tm, s, pitch)
            a_ref[:, sl] = (xs * rinv * g_ref[:, sl]).astype(a_ref.dtype)

    @pl.when(valid)
    def _():
        a = a_ref[...]
        g = jnp.dot(a, wg_ref[0], preferred_element_type=F32)
        u = jnp.dot(a, wu_ref[0], preferred_element_type=F32)
        o_ref[...] = (g * jax.nn.sigmoid(g) * u).astype(o_ref.dtype)

    @pl.when(jnp.logical_not(valid))
    def _():
        o_ref[...] = jnp.zeros(o_ref.shape, o_ref.dtype)


def grouped_gateup(x_slabs, g_pre, wg, wu, src_tok, tile_expert, n_valid, *, tm, tn=512):
    n_slots = src_tok.shape[0]
    K, N = wg.shape[1], wg.shape[2]
    pitch = _slab_pitch(K // LANES)
    tn = _tile(N, tn)
    nj = N // tn

    def wmap(t, j, tok, te, nv):
        return (te[t], 0, jnp.where(t < nv[0], j, nj - 1))

    wspec = pl.BlockSpec((1, K, tn), wmap)
    return pl.pallas_call(
        functools.partial(_grouped_gateup_kernel, tm=tm),
        grid_spec=pltpu.PrefetchScalarGridSpec(
            num_scalar_prefetch=3,
            grid=(n_slots // tm, nj),
            in_specs=[pl.BlockSpec(memory_space=pl.ANY),
                      pl.BlockSpec((1, K), lambda t, j, tok, te, nv: (0, 0)),
                      wspec, wspec],
            out_specs=pl.BlockSpec((tm, tn), lambda t, j, tok, te, nv: (t, j)),
            scratch_shapes=[pltpu.VMEM((tm * pitch, LANES), F32),
                            pltpu.VMEM((tm, K), BF16),
                            pltpu.SemaphoreType.DMA(())]),
        out_shape=jax.ShapeDtypeStruct((n_slots, N), BF16),
        compiler_params=_params("arbitrary", "arbitrary"),
        name="grouped_gateup",
    )(src_tok, tile_expert, n_valid, x_slabs, g_pre.reshape(1, K), wg, wu)


def _grouped_down_kernel(te_ref, nv_ref, a_ref, w_ref, gate_ref, o_ref):
    t = pl.program_id(0)

    @pl.when(t < nv_ref[0])
    def _():
        y = gate_ref[...] * jnp.dot(a_ref[...], w_ref[0], preferred_element_type=F32)
        for s in range(o_ref.shape[1]):
            o_ref[:, s, :] = y[:, s * LANES:(s + 1) * LANES]

    @pl.when(t >= nv_ref[0])
    def _():
        o_ref[...] = jnp.zeros(o_ref.shape, o_ref.dtype)


def grouped_down(act, wd, slot_gate, tile_expert, n_valid, *, tm, tn=1024):
    n_slots, K = act.shape
    N = wd.shape[2]
    tn = _tile(N, tn)
    nj = N // tn

    def wmap(t, j, te, nv):
        return (te[t], 0, jnp.where(t < nv[0], j, nj - 1))

    return pl.pallas_call(
        _grouped_down_kernel,
        grid_spec=pltpu.PrefetchScalarGridSpec(
            num_scalar_prefetch=2,
            grid=(n_slots // tm, nj),
            in_specs=[pl.BlockSpec((tm, K), lambda t, j, te, nv: (t, 0)),
                      pl.BlockSpec((1, K, tn), wmap),
                      pl.BlockSpec((tm, 1), lambda t, j, te, nv: (t, 0))],
            out_specs=pl.BlockSpec((tm, tn // LANES, LANES), lambda t, j, te, nv: (t, j, 0))),
        out_shape=jax.ShapeDtypeStruct((n_slots, N // LANES, LANES), F32),
        compiler_params=_params("arbitrary", "arbitrary"),
        name="grouped_down",
    )(tile_expert, n_valid, act, wd, slot_gate)


def _combine_kernel(dest_ref, y_hbm, x_ref, g_ref, o_ref, buf_ref, ysum_ref, sem, *, tr):
    base = pl.program_id(0) * tr
    d = o_ref.shape[1]
    chunks = d // LANES
    pitch = buf_ref.shape[1] // tr

    def copy(r, k):
        return pltpu.make_async_copy(
            _row_slab(y_hbm, dest_ref[(base + r) * TOP_K + k], chunks, chunks),
            _row_slab(buf_ref.at[k], r, chunks, pitch), sem)

    def start(r, carry):
        for k in range(TOP_K):
            copy(r, k).start()
        return carry

    def wait(r, carry):
        for k in range(TOP_K):
            copy(r, k).wait()
        return carry

    lax.fori_loop(0, tr, start, 0)
    lax.fori_loop(0, tr, wait, 0)

    ssq = jnp.zeros((tr, 1), F32)
    for s in range(chunks):
        ys = sum(_from_row_slabs(buf_ref.at[k], tr, s, pitch) for k in range(TOP_K))
        ysum_ref[:, s * LANES:(s + 1) * LANES] = ys
        ssq = ssq + jnp.sum(ys * ys, axis=-1, keepdims=True)
    rinv = lax.rsqrt(ssq / d + NORM_EPS)
    o_ref[...] = x_ref[...] + ysum_ref[...] * rinv * g_ref[...]


def combine(y_slabs, dest, x, g_post, *, tr=256):
    S, D = x.shape
    pitch = _slab_pitch(D // LANES)
    tr = _tile(S, tr, SUBLANES)
    return pl.pallas_call(
        functools.partial(_combine_kernel, tr=tr),
        grid_spec=pltpu.PrefetchScalarGridSpec(
            num_scalar_prefetch=1,
            grid=(S // tr,),
            in_specs=[pl.BlockSpec(memory_space=pl.ANY),
                      pl.BlockSpec((tr, D), lambda i, d: (i, 0)),
                      pl.BlockSpec((1, D), lambda i, d: (0, 0))],
            out_specs=pl.BlockSpec((tr, D), lambda i, d: (i, 0)),
            scratch_shapes=[pltpu.VMEM((TOP_K, tr * pitch, LANES), F32),
                            pltpu.VMEM((tr, D), F32),
                            pltpu.SemaphoreType.DMA(())]),
        out_shape=jax.ShapeDtypeStruct((S, D), F32),
        compiler_params=_params("arbitrary"),
        name="combine",
    )(dest, y_slabs, x, g_post.reshape(1, D))


def moe_ffn(x, x_slabs, h, g_pre, router_w, wg, wu, wd, g_post, *, tm):
    n_experts = router_w.shape[1]
    gates, idx = router(h, router_w)
    dest, src_tok, slot_gate, tile_expert, n_valid = dispatch_plan(idx, gates, n_experts, tm)
    act = grouped_gateup(x_slabs, g_pre, wg, wu, src_tok, tile_expert, n_valid, tm=tm)
    y = grouped_down(act, wd, slot_gate, tile_expert, n_valid, tm=tm)
    return combine(y.reshape(-1, LANES), dest, x, g_post)


def _pad_axis(w, axis, mult):
    pad = (-w.shape[axis]) % mult
    if pad == 0:
        return w
    widths = [(0, 0)] * w.ndim
    widths[axis] = (0, pad)
    return jnp.pad(w, widths)


def kernel(x, l0_norm_pre_mix, l0_w_in, l0_conv_w, l0_w_out, l0_norm_post_mix, l0_norm_pre_ffn, l0_ffn_w_gate, l0_ffn_w_up, l0_ffn_w_down, l0_norm_post_ffn, l1_norm_pre_mix, l1_w_qkv, l1_lambda_q1, l1_lambda_k1, l1_lambda_q2, l1_lambda_k2, l1_subln_g, l1_w_out, l1_norm_post_mix, l1_norm_pre_ffn, l1_router_w, l1_moe_w_gate, l1_moe_w_up, l1_moe_w_down, l1_norm_post_ffn):
    B, S, D = x.shape
    assert B == 1, "attention kernels handle a single sequence"
    a_width = D // 2
    b_width = D - a_width
    tables = rotary_tables(S)
    xr = x.reshape(S, D)

    h = rmsnorm_cast(xr, l0_norm_pre_mix)
    proj = proj_rotary(h, l0_w_in.astype(BF16), tables, a_width, F32)
    a_out = dilated_attention(proj, a_width)
    b_out = gated_conv(proj, l0_conv_w, 3 * a_width, b_width)
    mix = matmul(jnp.concatenate([a_out, b_out], axis=1), l0_w_out.astype(BF16), F32)
    xr, h = resid_norm(mix, xr, l0_norm_post_mix, l0_norm_pre_ffn)
    ff_tile = 1024
    wg = _pad_axis(l0_ffn_w_gate.astype(BF16), 1, ff_tile)
    wu = _pad_axis(l0_ffn_w_up.astype(BF16), 1, ff_tile)
    wd = _pad_axis(l0_ffn_w_down.astype(BF16), 0, ff_tile)
    act = gateup(h, wg, wu)
    ff = matmul(act, wd, F32, tk=_tile(wd.shape[0], 2816))
    xr, h = resid_norm(ff, xr, l0_norm_post_ffn, l1_norm_pre_mix)

    lambda_init = 0.8 - 0.6 * math.exp(-0.3 * 1)
    qkv = proj_rotary(h, l1_w_qkv.astype(BF16), tables, D, BF16)
    o_t = diff_attention(qkv, l1_lambda_q1, l1_lambda_k1, l1_lambda_q2, l1_lambda_k2,
                         l1_subln_g, D, lambda_init)
    mix = matmul(o_t, l1_w_out.astype(BF16), F32, lhs_transposed=True)
    xr, h, x_slabs = resid_norm(mix, xr, l1_norm_post_mix, l1_norm_pre_ffn, slabs=True)
    out = moe_ffn(xr, x_slabs, h, l1_norm_pre_ffn, l1_router_w, l1_moe_w_gate.astype(BF16),
                  l1_moe_w_up.astype(BF16), l1_moe_w_down.astype(BF16), l1_norm_post_ffn,
                  tm=_tile(S, 512, SUBLANES))
    return out.reshape(B, S, D)
```

```python
import functools
import math

import jax
import jax.numpy as jnp
from jax import lax
from jax.experimental import pallas as pl
from jax.experimental.pallas import tpu as pltpu

HEAD_DIM = 128
ROT_DIM = HEAD_DIM // 4
ROPE_THETA = 500000.0
QBLK = 128
A_BRANCHES = ((128, 1), (512, 4), (2048, 16))
DIL_UNIT = max(r for _, r in A_BRANCHES) * QBLK
CONV_WIDTH = 3
TOP_K = 2
NORM_EPS = 1e-6
ATTN_SCALE = HEAD_DIM ** -0.5

LANES = 128
SUBLANES = 8
VMEM_BYTES_V7X = 64 * 1024 * 1024
VMEM_LIMIT = VMEM_BYTES_V7X - 8 * 1024 * 1024

F32 = jnp.float32
BF16 = jnp.bfloat16


def _tile(n, pref, align=LANES):
    if n <= pref:
        return n
    t = (pref // align) * align
    while t >= align:
        if n % t == 0:
            return t
        t -= align
    raise ValueError(f"no {align}-aligned tile <= {pref} divides {n}")


def _params(*sem):
    return pltpu.CompilerParams(dimension_semantics=sem, vmem_limit_bytes=VMEM_LIMIT)


def _rms(x, g):
    ms = jnp.mean(x * x, axis=-1, keepdims=True)
    return x * lax.rsqrt(ms + NORM_EPS) * g


def _rmsnorm_kernel(x_ref, g_ref, h_ref):
    h_ref[...] = _rms(x_ref[...], g_ref[...]).astype(h_ref.dtype)


def rmsnorm_cast(x, g):
    S, D = x.shape
    tr = _tile(S, 256, SUBLANES)
    return pl.pallas_call(
        _rmsnorm_kernel,
        grid=(S // tr,),
        in_specs=[pl.BlockSpec((tr, D), lambda i: (i, 0)),
                  pl.BlockSpec((1, D), lambda i: (0, 0))],
        out_specs=pl.BlockSpec((tr, D), lambda i: (i, 0)),
        out_shape=jax.ShapeDtypeStruct((S, D), BF16),
        compiler_params=_params("parallel"),
        name="rmsnorm_cast",
    )(x, g.reshape(1, D))


def _slab_pitch(chunks):
    if chunks % SUBLANES or (chunks // SUBLANES) % 2:
        return chunks
    return chunks + SUBLANES


def _to_row_slabs(ref, val, pitch):
    rows, d = val.shape
    chunks = d // LANES
    for s in range(pitch):
        part = val[:, s * LANES:(s + 1) * LANES] if s < chunks else jnp.zeros((rows, LANES), val.dtype)
        ref[pl.ds(s, rows, stride=pitch), :] = part


def _from_row_slabs(ref, rows, s, pitch):
    return ref[pl.ds(s, rows, stride=pitch), :]


def _resid_norm_kernel(y_ref, x_ref, gpost_ref, gpre_ref, xo_ref, h_ref, *slab_ref, pitch):
    xn = x_ref[...] + _rms(y_ref[...], gpost_ref[...])
    xo_ref[...] = xn
    h_ref[...] = _rms(xn, gpre_ref[...]).astype(h_ref.dtype)
    if slab_ref:
        _to_row_slabs(slab_ref[0], xn, pitch)


def resid_norm(y, x, g_post, g_pre, *, slabs=False):
    S, D = x.shape
    tr = _tile(S, 256, SUBLANES)
    row = pl.BlockSpec((tr, D), lambda i: (i, 0))
    vec = pl.BlockSpec((1, D), lambda i: (0, 0))
    pitch = _slab_pitch(D // LANES)
    out_specs = [row, row]
    out_shape = [jax.ShapeDtypeStruct((S, D), F32), jax.ShapeDtypeStruct((S, D), BF16)]
    if slabs:
        out_specs.append(pl.BlockSpec((tr * pitch, LANES), lambda i: (i, 0)))
        out_shape.append(jax.ShapeDtypeStruct((S * pitch, LANES), F32))
    return pl.pallas_call(
        functools.partial(_resid_norm_kernel, pitch=pitch),
        grid=(S // tr,),
        in_specs=[row, row, vec, vec],
        out_specs=out_specs,
        out_shape=out_shape,
        compiler_params=_params("parallel"),
        name="resid_norm",
    )(y, x, g_post.reshape(1, D), g_pre.reshape(1, D))


def _mm_kernel(a_ref, w_ref, o_ref, *acc, nk, lhs_transposed):
    contract = 0 if lhs_transposed else 1
    part = lax.dot_general(a_ref[...], w_ref[...], (((contract,), (0,)), ((), ())),
                           preferred_element_type=F32)
    if nk == 1:
        o_ref[...] = part.astype(o_ref.dtype)
        return
    acc_ref, = acc
    k = pl.program_id(2)

    @pl.when(k == 0)
    def _():
        acc_ref[...] = part

    @pl.when(k > 0)
    def _():
        acc_ref[...] += part

    @pl.when(k == nk - 1)
    def _():
        o_ref[...] = acc_ref[...].astype(o_ref.dtype)


def matmul(a, w, out_dtype, *, tm=1024, tn=1024, tk=4096, lhs_transposed=False):
    M, K = a.shape[::-1] if lhs_transposed else a.shape
    N = w.shape[1]
    tm, tn, tk = _tile(M, tm, LANES if lhs_transposed else SUBLANES), _tile(N, tn), _tile(K, tk)
    nk = K // tk
    scratch = [pltpu.VMEM((tm, tn), F32)] if nk > 1 else []
    if lhs_transposed:
        a_spec = pl.BlockSpec((tk, tm), lambda i, j, k: (k, i))
    else:
        a_spec = pl.BlockSpec((tm, tk), lambda i, j, k: (i, k))
    return pl.pallas_call(
        functools.partial(_mm_kernel, nk=nk, lhs_transposed=lhs_transposed),
        grid=(M // tm, N // tn, nk),
        in_specs=[a_spec,
                  pl.BlockSpec((tk, tn), lambda i, j, k: (k, j))],
        out_specs=pl.BlockSpec((tm, tn), lambda i, j, k: (i, j)),
        out_shape=jax.ShapeDtypeStruct((M, N), out_dtype),
        scratch_shapes=scratch,
        compiler_params=_params("parallel", "parallel", "arbitrary"),
        name="matmul",
    )(a, w)


def _rotate(x, c, s_lo, s_hi):
    half = ROT_DIM // 2
    return (x * c + pltpu.roll(x, LANES - half, 1) * s_lo + pltpu.roll(x, half, 1) * s_hi)


def _proj_rot_kernel(a_ref, w_ref, c_ref, slo_ref, shi_ref, o_ref):
    acc = jnp.dot(a_ref[...], w_ref[...], preferred_element_type=F32)
    c, s_lo, s_hi = c_ref[0], slo_ref[0], shi_ref[0]
    for g in range(acc.shape[1] // LANES):
        sl = slice(g * LANES, (g + 1) * LANES)
        o_ref[:, sl] = _rotate(acc[:, sl], c, s_lo, s_hi).astype(o_ref.dtype)


def proj_rotary(a, w, tables, qk_width, out_dtype, *, tm=1024, tn=1024):
    M, K = a.shape
    N = w.shape[1]
    tm, tn = _tile(M, tm, SUBLANES), _tile(qk_width, tn)
    assert N % tn == 0
    qk_tiles = qk_width // tn
    tab = pl.BlockSpec((1, tm, LANES), lambda i, j: (jnp.minimum(j // qk_tiles, 2), i, 0))
    return pl.pallas_call(
        _proj_rot_kernel,
        grid=(M // tm, N // tn),
        in_specs=[pl.BlockSpec((tm, K), lambda i, j: (i, 0)),
                  pl.BlockSpec((K, tn), lambda i, j: (0, j)),
                  tab, tab, tab],
        out_specs=pl.BlockSpec((tm, tn), lambda i, j: (i, j)),
        out_shape=jax.ShapeDtypeStruct((M, N), out_dtype),
        compiler_params=_params("parallel", "parallel"),
        name="proj_rotary",
    )(a, w, *tables)


def rotary_tables(seq, q_scale):
    half = ROT_DIM // 2
    pos = jnp.arange(seq, dtype=F32)
    inv = ROPE_THETA ** (-jnp.arange(0, ROT_DIM, 2, dtype=F32) / ROT_DIM)
    ang = pos[:, None] * inv[None, :]
    cos, sin = jnp.cos(ang), jnp.sin(ang)
    zeros = jnp.zeros((seq, LANES - ROT_DIM), F32)
    z_half = jnp.zeros((seq, half), F32)
    c = jnp.concatenate([cos, cos, jnp.ones_like(zeros)], axis=1)
    s_lo = jnp.concatenate([-sin, z_half, zeros], axis=1)
    s_hi = jnp.concatenate([z_half, sin, zeros], axis=1)
    none = jnp.zeros_like(c)
    return (jnp.stack([c * q_scale, c, jnp.ones_like(c)]),
            jnp.stack([s_lo * q_scale, s_lo, none]),
            jnp.stack([s_hi * q_scale, s_hi, none]))


def _gateup_kernel(a_ref, wg_ref, wu_ref, o_ref):
    a = a_ref[...]
    g = jnp.dot(a, wg_ref[...], preferred_element_type=F32)
    u = jnp.dot(a, wu_ref[...], preferred_element_type=F32)
    o_ref[...] = (g * jax.nn.sigmoid(g) * u).astype(o_ref.dtype)


def gateup(a, wg, wu, *, tm=1024, tn=512):
    M, K = a.shape
    N = wg.shape[1]
    tm, tn = _tile(M, tm, SUBLANES), _tile(N, tn)
    wspec = pl.BlockSpec((K, tn), lambda i, j: (0, j))
    return pl.pallas_call(
        _gateup_kernel,
        grid=(M // tm, N // tn),
        in_specs=[pl.BlockSpec((tm, K), lambda i, j: (i, 0)), wspec, wspec],
        out_specs=pl.BlockSpec((tm, tn), lambda i, j: (i, j)),
        out_shape=jax.ShapeDtypeStruct((M, N), BF16),
        compiler_params=_params("parallel", "parallel"),
        name="gateup",
    )(a, wg, wu)


def _dilated_kernel(q_ref, kp_ref, k_ref, vp_ref, v_ref, o_ref, oacc_ref, lse_ref):
    u = pl.program_id(0)
    unit = q_ref.shape[0]
    i = lax.broadcasted_iota(jnp.int32, (QBLK, 2 * QBLK), 0)
    j = lax.broadcasted_iota(jnp.int32, (QBLK, 2 * QBLK), 1)
    dist = QBLK + i - j
    has_prev_unit = u > 0

    for b, (window, r) in enumerate(A_BRANCHES):
        n_back = window // r
        band = jnp.logical_and(dist >= 0, dist <= n_back)
        first_band = jnp.logical_and(band, jnp.logical_or(j >= QBLK, has_prev_unit))
        span = r * QBLK
        for n in range(unit // span):
            for c in range(r):
                own = pl.ds(n * span + c, QBLK, stride=r)
                if n == 0:
                    prev = pl.ds(unit - span + c, QBLK, stride=r)
                    k_prev, v_prev, mask = kp_ref[prev, :], vp_ref[prev, :], first_band
                else:
                    prev = pl.ds((n - 1) * span + c, QBLK, stride=r)
                    k_prev, v_prev, mask = k_ref[prev, :], v_ref[prev, :], band
                q = q_ref[own, :].astype(BF16)
                kk = jnp.concatenate([k_prev, k_ref[own, :]], axis=0).astype(BF16)
                vv = jnp.concatenate([v_prev, v_ref[own, :]], axis=0).astype(BF16)
                s = lax.dot_general(q, kk, (((1,), (1,)), ((), ())), preferred_element_type=F32)
                s = jnp.where(mask, s, -jnp.inf)
                m = jnp.max(s, axis=-1, keepdims=True)
                p = jnp.exp(s - m)
                den = jnp.sum(p, axis=-1, keepdims=True)
                o = jnp.dot(p.astype(BF16), vv, preferred_element_type=F32) / den
                oacc_ref[b, own, :] = o
                lse_ref[b, own, :] = jnp.broadcast_to(m + jnp.log(den), (QBLK, LANES))

    lses = [lse_ref[b] for b in range(len(A_BRANCHES))]
    top = functools.reduce(jnp.maximum, lses)
    wts = [jnp.exp(l - top) for l in lses]
    num = sum(w * oacc_ref[b] for b, w in enumerate(wts))
    o_ref[...] = (num / sum(wts)).astype(o_ref.dtype)


def dilated_attention(proj, a_width):
    S = proj.shape[0]
    heads = a_width // HEAD_DIM
    unit = DIL_UNIT
    assert S % unit == 0, "sequence must be a multiple of the largest dilation unit"

    def cur(off):
        return pl.BlockSpec((unit, HEAD_DIM), lambda u, h: (u, off + h))

    def prev(off):
        return pl.BlockSpec((unit, HEAD_DIM), lambda u, h: (jnp.maximum(u - 1, 0), off + h))

    nb = len(A_BRANCHES)
    return pl.pallas_call(
        _dilated_kernel,
        grid=(S // unit, heads),
        in_specs=[cur(0), prev(heads), cur(heads), prev(2 * heads), cur(2 * heads)],
        out_specs=pl.BlockSpec((unit, HEAD_DIM), lambda u, h: (u, h)),
        out_shape=jax.ShapeDtypeStruct((S, a_width), BF16),
        scratch_shapes=[pltpu.VMEM((nb, unit, HEAD_DIM), F32),
                        pltpu.VMEM((nb, unit, LANES), F32)],
        compiler_params=_params("parallel", "parallel"),
        name="dilated_attention",
    )(proj, proj, proj, proj, proj)


def _gated_conv_kernel(gb_ref, gc_ref, hb_ref, gcp_ref, hbp_ref, w_ref, o_ref):
    u = gc_ref[...] * hb_ref[...]
    up = gcp_ref[...] * hbp_ref[...]
    up = jnp.where(pl.program_id(0) > 0, up, 0.0)
    row = lax.broadcasted_iota(jnp.int32, up.shape, 0)
    y = w_ref[CONV_WIDTH - 1:CONV_WIDTH, :] * u
    for back in range(1, CONV_WIDTH):
        shifted = pltpu.roll(u, back, 0)
        top = jnp.where(row < back, pltpu.roll(up, back, 0), shifted[:SUBLANES])
        shifted = jnp.concatenate([top, shifted[SUBLANES:]], axis=0)
        y = y + w_ref[CONV_WIDTH - 1 - back:CONV_WIDTH - back, :] * shifted
    o_ref[...] = (gb_ref[...] * y).astype(o_ref.dtype)


def gated_conv(proj, conv_w, col0, b_width, *, tr=512, tc=512):
    S = proj.shape[0]
    tr, tc = _tile(S, tr, SUBLANES), _tile(b_width, tc)
    assert col0 % tc == 0
    nbw = b_width // tc
    rb = tr // SUBLANES

    def cur(part):
        return pl.BlockSpec((tr, tc), lambda i, j: (i, col0 // tc + part * nbw + j))

    def prev(part):
        return pl.BlockSpec((SUBLANES, tc),
                            lambda i, j: (jnp.maximum(i * rb - 1, 0), col0 // tc + part * nbw + j))

    return pl.pallas_call(
        _gated_conv_kernel,
        grid=(S // tr, nbw),
        in_specs=[cur(0), cur(1), cur(2), prev(1), prev(2),
                  pl.BlockSpec((CONV_WIDTH, tc), lambda i, j: (0, j))],
        out_specs=pl.BlockSpec((tr, tc), lambda i, j: (i, j)),
        out_shape=jax.ShapeDtypeStruct((S, b_width), BF16),
        compiler_params=_params("parallel", "parallel"),
        name="gated_conv",
    )(proj, proj, proj, proj, proj, conv_w)


def _diff_attn_kernel(q_ref, k_ref, vt_ref, lq1_ref, lk1_ref, lq2_ref, lk2_ref, g_ref, o_ref,
                      m_ref, l_ref, acc_ref, sa_ref, sb_ref, *, tq, lambda_init):
    qi = pl.program_id(1)
    m_ref[...] = jnp.full(m_ref.shape, -jnp.inf, F32)
    l_ref[...] = jnp.zeros(l_ref.shape, F32)
    acc_ref[...] = jnp.zeros(acc_ref.shape, F32)
    q = q_ref[...]
    qs = (q[:, :HEAD_DIM], q[:, HEAD_DIM:])

    def scores(kb, s_ref):
        k = k_ref[pl.ds(pl.multiple_of(kb * tq, tq), tq), :]
        for c in range(2):
            s_ref[c] = lax.dot_general(k[:, c * HEAD_DIM:(c + 1) * HEAD_DIM], qs[c],
                                       (((1,), (1,)), ((), ())), preferred_element_type=F32)

    def consume(kb, s_ref, diagonal):
        vt = vt_ref[0, kb]
        for c in range(2):
            s = s_ref[c]
            if diagonal:
                key = lax.broadcasted_iota(jnp.int32, s.shape, 0)
                qry = lax.broadcasted_iota(jnp.int32, s.shape, 1)
                s = jnp.where(key <= qry, s, -jnp.inf)
            m_prev = m_ref[c]
            m_new = jnp.maximum(m_prev, jnp.max(s, axis=0, keepdims=True))
            alpha = jnp.exp2(m_prev - m_new)
            p = jnp.exp2(s - m_new)
            l_ref[c] = alpha * l_ref[c] + jnp.sum(p, axis=0, keepdims=True)
            pv = jnp.dot(vt, p.astype(BF16), preferred_element_type=F32)
            acc_ref[c] = acc_ref[c] * alpha + pv
            m_ref[c] = m_new

    scores(0, sa_ref)

    def body(j, carry):
        scores(2 * j + 1, sb_ref)
        consume(2 * j, sa_ref, False)
        scores(2 * j + 2, sa_ref)
        consume(2 * j + 1, sb_ref, False)
        return carry

    lax.fori_loop(0, qi // 2, body, 0)

    @pl.when(qi % 2 == 1)
    def _():
        scores(qi, sb_ref)
        consume(qi - 1, sa_ref, False)
        consume(qi, sb_ref, True)

    @pl.when(qi % 2 == 0)
    def _():
        consume(qi, sa_ref, True)

    f32 = lambda r: r[...].astype(F32)
    lam = (jnp.exp(jnp.sum(f32(lq1_ref) * f32(lk1_ref), axis=-1, keepdims=True))
           - jnp.exp(jnp.sum(f32(lq2_ref) * f32(lk2_ref), axis=-1, keepdims=True)) + lambda_init)
    o = acc_ref[0] * (1.0 / l_ref[0]) - lam * (acc_ref[1] * (1.0 / l_ref[1]))
    ms = jnp.mean(o * o, axis=0, keepdims=True)
    o_ref[...] = (o * lax.rsqrt(ms + NORM_EPS) * g_ref[...] * (1.0 - lambda_init)).astype(o_ref.dtype)


def diff_attention(qkv, lq1, lk1, lq2, lk2, subln_g, d_model, lambda_init, *, tq=512):
    S = qkv.shape[0]
    vd = 2 * HEAD_DIM
    heads = d_model // vd
    tq = _tile(S, tq)
    nkb = S // tq
    vt = qkv[:, 2 * d_model:].reshape(nkb, tq, heads, vd).transpose(2, 0, 3, 1)
    vec = lambda n: pl.BlockSpec((1, n), lambda h, i: (0, 0))
    return pl.pallas_call(
        functools.partial(_diff_attn_kernel, tq=tq, lambda_init=lambda_init),
        grid=(heads, S // tq),
        in_specs=[pl.BlockSpec((tq, vd), lambda h, i: (i, h)),
                  pl.BlockSpec((S, vd), lambda h, i: (0, heads + h)),
                  pl.BlockSpec((1, nkb, vd, tq), lambda h, i: (h, 0, 0, 0)),
                  vec(HEAD_DIM), vec(HEAD_DIM), vec(HEAD_DIM), vec(HEAD_DIM),
                  pl.BlockSpec((vd, 1), lambda h, i: (0, 0))],
        out_specs=pl.BlockSpec((vd, tq), lambda h, i: (h, i)),
        out_shape=jax.ShapeDtypeStruct((d_model, S), BF16),
        scratch_shapes=[pltpu.VMEM((2, 1, tq), F32),
                        pltpu.VMEM((2, 1, tq), F32),
                        pltpu.VMEM((2, vd, tq), F32),
                        pltpu.VMEM((2, tq, tq), F32),
                        pltpu.VMEM((2, tq, tq), F32)],
        compiler_params=_params("parallel", "arbitrary"),
        name="diff_attention",
    )(qkv, qkv, vt, lq1.reshape(1, -1), lk1.reshape(1, -1), lq2.reshape(1, -1),
      lk2.reshape(1, -1), subln_g.reshape(-1, 1))


def _router_kernel(h_ref, w_ref, gate_ref, idx_ref, *, n_experts):
    logits = jnp.dot(h_ref[...], w_ref[...], preferred_element_type=F32)
    lane = lax.broadcasted_iota(jnp.int32, logits.shape, 1)
    logits = jnp.where(lane < n_experts, logits, -jnp.inf)
    v1 = jnp.max(logits, axis=-1, keepdims=True)
    i1 = jnp.min(jnp.where(logits == v1, lane, LANES), axis=-1, keepdims=True)
    rest = jnp.where(lane == i1, -jnp.inf, logits)
    v2 = jnp.max(rest, axis=-1, keepdims=True)
    i2 = jnp.min(jnp.where(rest == v2, lane, LANES), axis=-1, keepdims=True)
    e2 = jnp.exp(v2 - v1)
    g1 = 1.0 / (1.0 + e2)
    gate_ref[...] = jnp.where(lane == 0, g1, jnp.where(lane == 1, e2 * g1, 0.0))
    idx_ref[...] = jnp.where(lane == 0, i1, jnp.where(lane == 1, i2, 0))


def router(h, router_w):
    S, D = h.shape
    n_experts = router_w.shape[1]
    assert n_experts <= LANES
    w = jnp.zeros((D, LANES), BF16).at[:, :n_experts].set(router_w.astype(BF16))
    tr = _tile(S, 512, SUBLANES)
    out = pl.BlockSpec((tr, LANES), lambda i: (i, 0))
    gates, idx = pl.pallas_call(
        functools.partial(_router_kernel, n_experts=n_experts),
        grid=(S // tr,),
        in_specs=[pl.BlockSpec((tr, D), lambda i: (i, 0)),
                  pl.BlockSpec((D, LANES), lambda i: (0, 0))],
        out_specs=[out, out],
        out_shape=[jax.ShapeDtypeStruct((S, LANES), F32),
                   jax.ShapeDtypeStruct((S, LANES), jnp.int32)],
        compiler_params=_params("parallel"),
        name="router",
    )(h, w)
    return gates[:, :TOP_K], idx[:, :TOP_K]


def dispatch_plan(idx, gates, n_experts, tm):
    S = idx.shape[0]
    pairs = S * TOP_K
    n_slots = pairs + n_experts * tm
    n_tiles = n_slots // tm
    e_flat = idx.reshape(pairs)
    onehot = (e_flat[:, None] == jnp.arange(n_experts, dtype=jnp.int32)[None, :]).astype(jnp.int32)
    rank = jnp.take_along_axis(jnp.cumsum(onehot, axis=0) - onehot, e_flat[:, None], axis=1)[:, 0]
    tiles_per_e = (jnp.sum(onehot, axis=0) + tm - 1) // tm
    tile_end = jnp.cumsum(tiles_per_e)
    slot0 = (tile_end - tiles_per_e) * tm
    dest = (slot0[e_flat] + rank).astype(jnp.int32)
    src_tok = jnp.zeros((n_slots,), jnp.int32).at[dest].set(jnp.arange(pairs, dtype=jnp.int32) // TOP_K)
    slot_gate = jnp.zeros((n_slots,), F32).at[dest].set(gates.reshape(pairs))
    tile_ids = jnp.arange(n_tiles, dtype=jnp.int32)
    tile_expert = jnp.minimum(jnp.sum((tile_end[None, :] <= tile_ids[:, None]).astype(jnp.int32), axis=1),
                              n_experts - 1).astype(jnp.int32)
    n_valid = tile_end[-1:].astype(jnp.int32)
    return dest, src_tok, slot_gate.reshape(n_slots, 1), tile_expert, n_valid, tiles_per_e


def _row_slab(ref, row, chunks, pitch):
    align = math.gcd(pitch, SUBLANES) if pitch % SUBLANES else SUBLANES
    return ref.at[pl.ds(pl.multiple_of(row * pitch, align), chunks), :]


def _gather_norm_kernel(tok_ref, nv_ref, x_hbm, g_ref, o_ref, buf_ref, sem, *, tm):
    t = pl.program_id(0)
    d = o_ref.shape[1]
    chunks = d // LANES
    pitch = buf_ref.shape[0] // tm

    @pl.when(t < nv_ref[0])
    def _():
        base = t * tm

        def copy(r):
            return pltpu.make_async_copy(_row_slab(x_hbm, tok_ref[base + r], chunks, pitch),
                                         _row_slab(buf_ref, r, chunks, pitch), sem)

        def start(r, carry):
            copy(r).start()
            return carry

        def wait(r, carry):
            copy(r).wait()
            return carry

        lax.fori_loop(0, tm, start, 0)
        lax.fori_loop(0, tm, wait, 0)
        ssq = jnp.zeros((tm, 1), F32)
        for s in range(chunks):
            xs = _from_row_slabs(buf_ref, tm, s, pitch)
            ssq = ssq + jnp.sum(xs * xs, axis=-1, keepdims=True)
        rinv = lax.rsqrt(ssq / d + NORM_EPS)
        for s in range(chunks):
            sl = slice(s * LANES, (s + 1) * LANES)
            xs = _from_row_slabs(buf_ref, tm, s, pitch)
            o_ref[:, sl] = (xs * rinv * g_ref[:, sl]).astype(o_ref.dtype)

    @pl.when(t >= nv_ref[0])
    def _():
        o_ref[...] = jnp.zeros(o_ref.shape, o_ref.dtype)


def gather_norm(x_slabs, g_pre, src_tok, n_valid, *, tm):
    n_slots = src_tok.shape[0]
    D = g_pre.shape[0]
    pitch = _slab_pitch(D // LANES)
    return pl.pallas_call(
        functools.partial(_gather_norm_kernel, tm=tm),
        grid_spec=pltpu.PrefetchScalarGridSpec(
            num_scalar_prefetch=2,
            grid=(n_slots // tm,),
            in_specs=[pl.BlockSpec(memory_space=pl.ANY),
                      pl.BlockSpec((1, D), lambda t, tok, nv: (0, 0))],
            out_specs=pl.BlockSpec((tm, D), lambda t, tok, nv: (t, 0)),
            scratch_shapes=[pltpu.VMEM((tm * pitch, LANES), F32),
                            pltpu.SemaphoreType.DMA(())]),
        out_shape=jax.ShapeDtypeStruct((n_slots, D), BF16),
        compiler_params=_params("arbitrary"),
        name="gather_norm",
    )(src_tok, n_valid, x_slabs, g_pre.reshape(1, D))


def weight_major_schedule(tiles_per_e, n_tiles, nj):
    n_experts = tiles_per_e.shape[0]
    tile_end = jnp.cumsum(tiles_per_e)
    tile_start = tile_end - tiles_per_e
    n_used = tile_end[-1]
    s = jnp.arange(n_tiles * nj, dtype=jnp.int32)
    used = s < n_used * nj
    e = jnp.minimum(jnp.sum((tile_end[None, :] * nj <= s[:, None]).astype(jnp.int32), axis=1),
                    n_experts - 1)
    n_e = jnp.maximum(tiles_per_e[e], 1)
    r = s - tile_start[e] * nj
    tail = s - n_used * nj
    last_e = jnp.maximum(jnp.sum((tile_start < n_used).astype(jnp.int32)) - 1, 0)
    tile = jnp.where(used, tile_start[e] + r % n_e, n_used + tail // nj)
    col = jnp.where(used, r // n_e, tail % nj)
    w_e = jnp.where(used, e, last_e)
    w_col = jnp.where(used, r // n_e, nj - 1)
    flags = jnp.where(used, 1 + 2 * (r % n_e == 0).astype(jnp.int32), 0)
    as_i32 = lambda a: a.astype(jnp.int32)
    return as_i32(tile), as_i32(col), as_i32(w_e), as_i32(w_col), as_i32(flags)


def _grouped_gateup_kernel(tile_ref, col_ref, we_ref, wcol_ref, flag_ref, a_ref, wg_ref, wu_ref,
                           o_ref, wgb_ref, wub_ref):
    flag = flag_ref[pl.program_id(0)]

    @pl.when(flag >= 2)
    def _():
        wgb_ref[...] = wg_ref[0].astype(wgb_ref.dtype)
        wub_ref[...] = wu_ref[0].astype(wub_ref.dtype)

    @pl.when(flag > 0)
    def _():
        a = a_ref[...]
        g = jnp.dot(a, wgb_ref[...], preferred_element_type=F32)
        u = jnp.dot(a, wub_ref[...], preferred_element_type=F32)
        o_ref[...] = (g * jax.nn.sigmoid(g) * u).astype(o_ref.dtype)

    @pl.when(flag == 0)
    def _():
        o_ref[...] = jnp.zeros(o_ref.shape, o_ref.dtype)


def grouped_gateup(xs, wg, wu, tiles_per_e, *, tm, tn=512):
    n_slots, K = xs.shape
    N = wg.shape[2]
    tn = _tile(N, tn)
    schedule = weight_major_schedule(tiles_per_e, n_slots // tm, N // tn)
    wspec = pl.BlockSpec((1, K, tn), lambda s, tile, col, we, wcol, flag: (we[s], 0, wcol[s]))
    return pl.pallas_call(
        _grouped_gateup_kernel,
        grid_spec=pltpu.PrefetchScalarGridSpec(
            num_scalar_prefetch=5,
            grid=(schedule[0].shape[0],),
            in_specs=[pl.BlockSpec((tm, K), lambda s, tile, col, we, wcol, flag: (tile[s], 0)),
                      wspec, wspec],
            out_specs=pl.BlockSpec((tm, tn), lambda s, tile, col, we, wcol, flag: (tile[s], col[s])),
            scratch_shapes=[pltpu.VMEM((K, tn), BF16), pltpu.VMEM((K, tn), BF16)]),
        out_shape=jax.ShapeDtypeStruct((n_slots, N), BF16),
        compiler_params=_params("arbitrary"),
        name="grouped_gateup",
    )(*schedule, xs, wg, wu)


def _grouped_down_kernel(te_ref, nv_ref, a_ref, w_ref, gate_ref, o_ref):
    t = pl.program_id(0)

    @pl.when(t < nv_ref[0])
    def _():
        y = gate_ref[...] * jnp.dot(a_ref[...], w_ref[0], preferred_element_type=F32)
        for s in range(o_ref.shape[1]):
            o_ref[:, s, :] = y[:, s * LANES:(s + 1) * LANES]

    @pl.when(t >= nv_ref[0])
    def _():
        o_ref[...] = jnp.zeros(o_ref.shape, o_ref.dtype)


def grouped_down(act, wd, slot_gate, tile_expert, n_valid, *, tm, tn=1024):
    n_slots, K = act.shape
    N = wd.shape[2]
    tn = _tile(N, tn)
    nj = N // tn

    def wmap(t, j, te, nv):
        return (te[t], 0, jnp.where(t < nv[0], j, nj - 1))

    return pl.pallas_call(
        _grouped_down_kernel,
        grid_spec=pltpu.PrefetchScalarGridSpec(
            num_scalar_prefetch=2,
            grid=(n_slots // tm, nj),
            in_specs=[pl.BlockSpec((tm, K), lambda t, j, te, nv: (t, 0)),
                      pl.BlockSpec((1, K, tn), wmap),
                      pl.BlockSpec((tm, 1), lambda t, j, te, nv: (t, 0))],
            out_specs=pl.BlockSpec((tm, tn // LANES, LANES), lambda t, j, te, nv: (t, j, 0))),
        out_shape=jax.ShapeDtypeStruct((n_slots, N // LANES, LANES), F32),
        compiler_params=_params("arbitrary", "arbitrary"),
        name="grouped_down",
    )(tile_expert, n_valid, act, wd, slot_gate)


def _combine_kernel(dest_ref, y_hbm, x_ref, g_ref, o_ref, buf_ref, ysum_ref, sem, *, tr):
    base = pl.program_id(0) * tr
    d = o_ref.shape[1]
    chunks = d // LANES
    pitch = buf_ref.shape[1] // tr

    def copy(r, k):
        return pltpu.make_async_copy(
            _row_slab(y_hbm, dest_ref[(base + r) * TOP_K + k], chunks, chunks),
            _row_slab(buf_ref.at[k], r, chunks, pitch), sem)

    def start(r, carry):
        for k in range(TOP_K):
            copy(r, k).start()
        return carry

    def wait(r, carry):
        for k in range(TOP_K):
            copy(r, k).wait()
        return carry

    lax.fori_loop(0, tr, start, 0)
    lax.fori_loop(0, tr, wait, 0)

    ssq = jnp.zeros((tr, 1), F32)
    for s in range(chunks):
        ys = sum(_from_row_slabs(buf_ref.at[k], tr, s, pitch) for k in range(TOP_K))
        ysum_ref[:, s * LANES:(s + 1) * LANES] = ys
        ssq = ssq + jnp.sum(ys * ys, axis=-1, keepdims=True)
    rinv = lax.rsqrt(ssq / d + NORM_EPS)
    o_ref[...] = x_ref[...] + ysum_ref[...] * rinv * g_ref[...]


def combine(y_slabs, dest, x, g_post, *, tr=256):
    S, D = x.shape
    pitch = _slab_pitch(D // LANES)
    tr = _tile(S, tr, SUBLANES)
    return pl.pallas_call(
        functools.partial(_combine_kernel, tr=tr),
        grid_spec=pltpu.PrefetchScalarGridSpec(
            num_scalar_prefetch=1,
            grid=(S // tr,),
            in_specs=[pl.BlockSpec(memory_space=pl.ANY),
                      pl.BlockSpec((tr, D), lambda i, d: (i, 0)),
                      pl.BlockSpec((1, D), lambda i, d: (0, 0))],
            out_specs=pl.BlockSpec((tr, D), lambda i, d: (i, 0)),
            scratch_shapes=[pltpu.VMEM((TOP_K, tr * pitch, LANES), F32),
                            pltpu.VMEM((tr, D), F32),
                            pltpu.SemaphoreType.DMA(())]),
        out_shape=jax.ShapeDtypeStruct((S, D), F32),
        compiler_params=_params("arbitrary"),
        name="combine",
    )(dest, y_slabs, x, g_post.reshape(1, D))


def moe_ffn(x, x_slabs, h, g_pre, router_w, wg, wu, wd, g_post, *, tm):
    n_experts = router_w.shape[1]
    gates, idx = router(h, router_w)
    dest, src_tok, slot_gate, tile_expert, n_valid, tiles_per_e = dispatch_plan(
        idx, gates, n_experts, tm)
    xs = gather_norm(x_slabs, g_pre, src_tok, n_valid, tm=tm)
    act = grouped_gateup(xs, wg, wu, tiles_per_e, tm=tm)
    y = grouped_down(act, wd, slot_gate, tile_expert, n_valid, tm=tm)
    return combine(y.reshape(-1, LANES), dest, x, g_post)


def _pad_axis(w, axis, mult):
    pad = (-w.shape[axis]) % mult
    if pad == 0:
        return w
    widths = [(0, 0)] * w.ndim
    widths[axis] = (0, pad)
    return jnp.pad(w, widths)


def kernel(x, l0_norm_pre_mix, l0_w_in, l0_conv_w, l0_w_out, l0_norm_post_mix, l0_norm_pre_ffn, l0_ffn_w_gate, l0_ffn_w_up, l0_ffn_w_down, l0_norm_post_ffn, l1_norm_pre_mix, l1_w_qkv, l1_lambda_q1, l1_lambda_k1, l1_lambda_q2, l1_lambda_k2, l1_subln_g, l1_w_out, l1_norm_post_mix, l1_norm_pre_ffn, l1_router_w, l1_moe_w_gate, l1_moe_w_up, l1_moe_w_down, l1_norm_post_ffn):
    B, S, D = x.shape
    assert B == 1, "attention kernels handle a single sequence"
    a_width = D // 2
    b_width = D - a_width
    xr = x.reshape(S, D)

    h = rmsnorm_cast(xr, l0_norm_pre_mix)
    proj = proj_rotary(h, l0_w_in.astype(BF16), rotary_tables(S, ATTN_SCALE), a_width, F32)
    a_out = dilated_attention(proj, a_width)
    b_out = gated_conv(proj, l0_conv_w, 3 * a_width, b_width)
    mix = matmul(jnp.concatenate([a_out, b_out], axis=1), l0_w_out.astype(BF16), F32)
    xr, h = resid_norm(mix, xr, l0_norm_post_mix, l0_norm_pre_ffn)
    ff_tile = 1024
    wg = _pad_axis(l0_ffn_w_gate.astype(BF16), 1, ff_tile)
    wu = _pad_axis(l0_ffn_w_up.astype(BF16), 1, ff_tile)
    wd = _pad_axis(l0_ffn_w_down.astype(BF16), 0, ff_tile)
    act = gateup(h, wg, wu)
    ff = matmul(act, wd, F32, tk=_tile(wd.shape[0], 2816))
    xr, h = resid_norm(ff, xr, l0_norm_post_ffn, l1_norm_pre_mix)

    lambda_init = 0.8 - 0.6 * math.exp(-0.3 * 1)
    qkv = proj_rotary(h, l1_w_qkv.astype(BF16), rotary_tables(S, ATTN_SCALE * math.log2(math.e)),
                      D, BF16)
    o_t = diff_attention(qkv, l1_lambda_q1, l1_lambda_k1, l1_lambda_q2, l1_lambda_k2,
                         l1_subln_g, D, lambda_init)
    mix = matmul(o_t, l1_w_out.astype(BF16), F32, lhs_transposed=True)
    xr, h, x_slabs = resid_norm(mix, xr, l1_norm_post_mix, l1_norm_pre_ffn, slabs=True)
    out = moe_ffn(xr, x_slabs, h, l1_norm_pre_ffn, l1_router_w, l1_moe_w_gate, l1_moe_w_up,
                  l1_moe_w_down.astype(BF16), l1_norm_post_ffn, tm=_tile(S, 512, SUBLANES))
    return out.reshape(B, S, D)
```
